```python
import math
import jax, jax.numpy as jnp
from jax import lax
import numpy as np

D_MODEL = 2048
BATCH = 32
SEQ = 256
DEPTH = 2
DEC_BATCH = 4
DEC_SEQ = 4096
PAST_LEN = 256

GRID_W = 64
N_MIXERS = 2
N_LAYERS_A = (DEPTH + 1) // 2
N_LAYERS_B = DEPTH // 2
HEADS_A = 16
QK_DIM_A = 64
V_DIM_A = 2 * QK_DIM_A
Q_HEADS_B = 32
KV_HEADS_B = 4
GROUP_B = Q_HEADS_B // KV_HEADS_B
HEAD_DIM_B = 64
WINDOW = 128
Q_BLOCK = 128
N_EXPERTS = 32
TOP_K = 4
D_FF = D_MODEL
SWIGLU_LIMIT = 7.0
SWIGLU_ALPHA = 1.702
MOE_BLOCK = 128
ROPE_BASE = 10000.0
NORM_EPS = 1e-6
NEG_INF = -1e30

kernel_name = 'hybrid_diffattn_swa_moe_denoise_step'


def rms_norm(x, g):
    xf = x.astype(jnp.float32)
    xf = xf * lax.rsqrt(jnp.mean(xf * xf, axis=-1, keepdims=True) + NORM_EPS)
    return xf.astype(x.dtype) * g


def adaln_params(cond, w, b):
    mod = jax.nn.silu(cond) @ w + b
    return jnp.split(mod[:, None, :], 6, axis=-1)


def modulate(x, g, shift, scale):
    return rms_norm(x, g) * (1.0 + scale) + shift


def axial_rope_tables(n_tokens, dim):
    rows = n_tokens // GRID_W
    row = jnp.broadcast_to(jnp.arange(rows, dtype=jnp.float32)[:, None], (rows, GRID_W)).reshape(-1)
    col = jnp.broadcast_to(jnp.arange(GRID_W, dtype=jnp.float32)[None, :], (rows, GRID_W)).reshape(-1)
    quarter = dim // 4
    inv = ROPE_BASE ** (-jnp.arange(quarter, dtype=jnp.float32) / quarter)
    ang_r = row[:, None] * inv[None, :]
    ang_c = col[:, None] * inv[None, :]
    ang = jnp.concatenate([ang_r, ang_r, ang_c, ang_c], axis=-1)
    return jnp.cos(ang), jnp.sin(ang)


def apply_axial_rope(x, cos, sin):
    shape = (1, x.shape[1]) + (1,) * (x.ndim - 3) + (x.shape[-1],)
    cos = cos.reshape(shape).astype(x.dtype)
    sin = sin.reshape(shape).astype(x.dtype)
    r1, r2, c1, c2 = jnp.split(x, 4, axis=-1)
    rot = jnp.concatenate([-r2, r1, -c2, c1], axis=-1)
    return x * cos + rot * sin


def to_query_blocks(x):
    b, n = x.shape[:2]
    return jnp.moveaxis(x.reshape((b, n // Q_BLOCK, Q_BLOCK) + x.shape[2:]), 1, 0)


def from_query_blocks(y):
    nb, b = y.shape[:2]
    return jnp.moveaxis(y, 0, 1).reshape((b, nb * Q_BLOCK) + y.shape[3:])


def band_blocks(x):
    b, n = x.shape[:2]
    nb = n // Q_BLOCK
    pad = [(0, 0), (Q_BLOCK, Q_BLOCK)] + [(0, 0)] * (x.ndim - 2)
    xp = jnp.pad(x, pad).reshape((b, nb + 2, Q_BLOCK) + x.shape[2:])
    band = jnp.concatenate([xp[:, :-2], xp[:, 1:-1], xp[:, 2:]], axis=2)
    return jnp.moveaxis(band, 1, 0)


def diff_attn_project(h, w_qkv, q_gain, k_gain):
    b, n, _ = h.shape
    qk_width = HEADS_A * 2 * QK_DIM_A
    q, k, v = jnp.split(h @ w_qkv, [qk_width, 2 * qk_width], axis=-1)
    q = rms_norm(q.reshape(b, n, HEADS_A, 2, QK_DIM_A), q_gain)
    k = rms_norm(k.reshape(b, n, HEADS_A, 2, QK_DIM_A), k_gain)
    return q, k, v.reshape(b, n, HEADS_A, V_DIM_A)


def diff_attn_block(q, k, v, lam):
    s = jnp.einsum('bqhmd,bkhmd->bhmqk', q, k).astype(jnp.float32) * (QK_DIM_A ** -0.5)
    p = jax.nn.softmax(s, axis=-1)
    attn = p[:, :, 0] - lam * p[:, :, 1]
    return jnp.einsum('bhqk,bkhd->bqhd', attn.astype(v.dtype), v)


def diff_attention(q, k, v, lam):
    out = lax.map(lambda qb: diff_attn_block(qb, k, v, lam), to_query_blocks(q))
    return from_query_blocks(out)


def diff_attn_merge(o, sub_gain, lambda_init, w_o):
    b, n = o.shape[:2]
    o = rms_norm(o, sub_gain) * (1.0 - lambda_init)
    return o.reshape(b, n, HEADS_A * V_DIM_A) @ w_o


def gqa_project(h, w_qkv, q_gain, k_gain):
    b, n, _ = h.shape
    q_width = Q_HEADS_B * HEAD_DIM_B
    kv_width = KV_HEADS_B * HEAD_DIM_B
    q, k, v = jnp.split(h @ w_qkv, [q_width, q_width + kv_width], axis=-1)
    q = rms_norm(q.reshape(b, n, KV_HEADS_B, GROUP_B, HEAD_DIM_B), q_gain)
    k = rms_norm(k.reshape(b, n, KV_HEADS_B, HEAD_DIM_B), k_gain)
    return q, k, v.reshape(b, n, KV_HEADS_B, HEAD_DIM_B)


def sink_gqa_block(q, k, v, mask, sink):
    s = jnp.einsum('bqhgd,bkhd->bhgqk', q, k).astype(jnp.float32) * (HEAD_DIM_B ** -0.5)
    s = jnp.where(mask, s, NEG_INF)
    sk = sink.astype(jnp.float32).reshape(KV_HEADS_B, GROUP_B, 1)
    m = jnp.maximum(jnp.max(s, axis=-1), sk)
    p = jnp.exp(s - m[..., None])
    denom = jnp.sum(p, axis=-1) + jnp.exp(sk - m)
    w = p / denom[..., None]
    return jnp.einsum('bhgqk,bkhd->bqhgd', w.astype(v.dtype), v)


def context_sink_attention(q, k, v, sink):
    mask = jnp.ones((Q_BLOCK, k.shape[1]), dtype=bool)
    out = lax.map(lambda qb: sink_gqa_block(qb, k, v, mask, sink), to_query_blocks(q))
    return from_query_blocks(out)


def latent_window_sink_attention(q, k, v, k_ctx, v_ctx, sink):
    n = q.shape[1]
    nb = n // Q_BLOCK
    kb = band_blocks(k)
    vb = band_blocks(v)
    qpos = jnp.arange(Q_BLOCK)[:, None]
    kloc = jnp.arange(3 * Q_BLOCK)[None, :]
    rel_ok = jnp.abs(qpos - kloc + Q_BLOCK) <= WINDOW
    kpos = (jnp.arange(nb)[:, None] - 1) * Q_BLOCK + kloc
    in_range = (kpos >= 0) & (kpos < n)
    ctx_ok = jnp.ones((Q_BLOCK, k_ctx.shape[1]), dtype=bool)

    def block(args):
        qb, kbb, vbb, ok = args
        keys = jnp.concatenate([k_ctx, kbb], axis=1)
        vals = jnp.concatenate([v_ctx, vbb], axis=1)
        mask = jnp.concatenate([ctx_ok, rel_ok & ok[None, :]], axis=1)
        return sink_gqa_block(qb, keys, vals, mask, sink)

    out = lax.map(block, (to_query_blocks(q), kb, vb, in_range))
    return from_query_blocks(out)


def gqa_merge(o, w_o):
    b, n = o.shape[:2]
    return o.reshape(b, n, Q_HEADS_B * HEAD_DIM_B) @ w_o


def moe_ffn(x, w_router, b_router, w_in, b_in, w_out, b_out):
    b, n, d = x.shape
    t = b * n
    xs = x.reshape(t, d)
    logits = (xs @ w_router + b_router).astype(jnp.float32)
    top_logit, top_idx = lax.top_k(logits, TOP_K)
    gates = jax.nn.softmax(top_logit, axis=-1).astype(x.dtype)
    n_assign = t * TOP_K
    expert_of = top_idx.reshape(n_assign).astype(jnp.int32)
    token_of = jnp.arange(n_assign, dtype=jnp.int32) // TOP_K
    gate_of = gates.reshape(n_assign)
    order = jnp.argsort(expert_of)
    e_sorted = expert_of[order]
    counts = jnp.bincount(expert_of, length=N_EXPERTS).astype(jnp.int32)
    padded = (counts + MOE_BLOCK - 1) // MOE_BLOCK * MOE_BLOCK
    start = jnp.cumsum(counts) - counts
    padded_end = jnp.cumsum(padded)
    padded_start = padded_end - padded
    slot = padded_start[e_sorted] + jnp.arange(n_assign, dtype=jnp.int32) - start[e_sorted]
    capacity = -(-(n_assign + N_EXPERTS * (MOE_BLOCK - 1)) // MOE_BLOCK) * MOE_BLOCK
    n_blocks = capacity // MOE_BLOCK
    slot_token = jnp.full((capacity,), t, dtype=jnp.int32).at[slot].set(token_of[order])
    slot_gate = jnp.zeros((capacity,), x.dtype).at[slot].set(gate_of[order])
    block_start = jnp.arange(n_blocks, dtype=jnp.int32) * MOE_BLOCK
    block_expert = jnp.minimum(jnp.searchsorted(padded_end, block_start, side='right'), N_EXPERTS - 1)
    x_pad = jnp.concatenate([xs, jnp.zeros((1, d), x.dtype)], axis=0)
    xb = x_pad[slot_token].reshape(n_blocks, MOE_BLOCK, d)

    def expert_block(args):
        xe, e = args
        h = xe @ w_in[e] + b_in[e]
        g, u = jnp.split(h, 2, axis=-1)
        g = jnp.minimum(g, SWIGLU_LIMIT)
        u = jnp.clip(u, -SWIGLU_LIMIT, SWIGLU_LIMIT)
        act = g * jax.nn.sigmoid(SWIGLU_ALPHA * g) * (u + 1.0)
        return act @ w_out[e] + b_out[e]

    yb = lax.map(expert_block, (xb, block_expert))
    y = yb.reshape(capacity, d) * slot_gate[:, None]
    out = jnp.zeros((t + 1, d), x.dtype).at[slot_token].add(y)[:t]
    return out.reshape(b, n, d)


def setup_inputs(seed: int = 0) -> dict:
    key = jax.random.key(seed)
    ks = iter(jax.random.split(key, 48))

    def nrm(shape, scale):
        return jax.random.normal(next(ks), shape, jnp.float32) * scale

    def gain(shape):
        return 1.0 + nrm(shape, 0.1)

    D = D_MODEL
    qkv_a = HEADS_A * (4 * QK_DIM_A + V_DIM_A)
    qkv_b = (Q_HEADS_B + 2 * KV_HEADS_B) * HEAD_DIM_B
    return {
        'x_prompt': nrm((BATCH, SEQ, D), 1.0),
        'x_sample': nrm((DEC_BATCH, DEC_SEQ, D), 1.0),
        'cache_a_k': nrm((DEC_BATCH, N_LAYERS_A, PAST_LEN, HEADS_A, 2, QK_DIM_A), 1.0),
        'cache_a_v': nrm((DEC_BATCH, N_LAYERS_A, PAST_LEN, HEADS_A, V_DIM_A), 1.0),
        'cache_b_k': nrm((DEC_BATCH, N_LAYERS_B, PAST_LEN, KV_HEADS_B, HEAD_DIM_B), 1.0),
        'cache_b_v': nrm((DEC_BATCH, N_LAYERS_B, PAST_LEN, KV_HEADS_B, HEAD_DIM_B), 1.0),
        'c': nrm((DEC_BATCH, D), 1.0),
        'c_ctx': nrm((D,), 1.0),
        'w_ada': nrm((DEPTH, D, 6 * D), 0.5 * D ** -0.5),
        'b_ada': nrm((DEPTH, 6 * D), 0.02),
        'g_mix': gain((DEPTH, D)),
        'g_ffn': gain((DEPTH, D)),
        'w_qkv_a': nrm((N_LAYERS_A, D, qkv_a), D ** -0.5),
        'q_norm_a': gain((N_LAYERS_A, QK_DIM_A)),
        'k_norm_a': gain((N_LAYERS_A, QK_DIM_A)),
        'lam_q1_a': nrm((N_LAYERS_A, QK_DIM_A), 0.1),
        'lam_k1_a': nrm((N_LAYERS_A, QK_DIM_A), 0.1),
        'lam_q2_a': nrm((N_LAYERS_A, QK_DIM_A), 0.1),
        'lam_k2_a': nrm((N_LAYERS_A, QK_DIM_A), 0.1),
        'subln_a': gain((N_LAYERS_A, V_DIM_A)),
        'w_o_a': nrm((N_LAYERS_A, HEADS_A * V_DIM_A, D), (HEADS_A * V_DIM_A) ** -0.5),
        'w_qkv_b': nrm((N_LAYERS_B, D, qkv_b), D ** -0.5),
        'q_norm_b': gain((N_LAYERS_B, HEAD_DIM_B)),
        'k_norm_b': gain((N_LAYERS_B, HEAD_DIM_B)),
        'sinks_b': nrm((N_LAYERS_B, Q_HEADS_B), 0.5),
        'w_o_b': nrm((N_LAYERS_B, Q_HEADS_B * HEAD_DIM_B, D), (Q_HEADS_B * HEAD_DIM_B) ** -0.5),
        'w_router': nrm((DEPTH, D, N_EXPERTS), D ** -0.5),
        'b_router': nrm((DEPTH, N_EXPERTS), 0.01),
        'w_in': nrm((DEPTH, N_EXPERTS, D, 2 * D_FF), D ** -0.5),
        'b_in': nrm((DEPTH, N_EXPERTS, 2 * D_FF), 0.02),
        'w_out': nrm((DEPTH, N_EXPERTS, D_FF, D), D_FF ** -0.5),
        'b_out': nrm((DEPTH, N_EXPERTS, D), 0.02),
    }


def reference(x_prompt, x_sample, cache_a_k, cache_a_v, cache_b_k, cache_b_v, c, c_ctx,
              w_ada, b_ada, g_mix, g_ffn,
              w_qkv_a, q_norm_a, k_norm_a, lam_q1_a, lam_k1_a, lam_q2_a, lam_k2_a, subln_a, w_o_a,
              w_qkv_b, q_norm_b, k_norm_b, sinks_b, w_o_b,
              w_router, b_router, w_in, b_in, w_out, b_out):
    x_p = x_prompt
    x_s = x_sample
    n_lat = x_s.shape[1]
    cos_a, sin_a = axial_rope_tables(n_lat, QK_DIM_A)
    cos_b, sin_b = axial_rope_tables(n_lat, HEAD_DIM_B)
    new_a_k, new_a_v, new_b_k, new_b_v = [], [], [], []

    for layer in range(DEPTH):
        sh_mp, sc_mp, gt_mp, sh_fp, sc_fp, gt_fp = adaln_params(c_ctx[None, :], w_ada[layer], b_ada[layer])
        sh_ms, sc_ms, gt_ms, sh_fs, sc_fs, gt_fs = adaln_params(c, w_ada[layer], b_ada[layer])
        h_p = modulate(x_p, g_mix[layer], sh_mp, sc_mp)
        h_s = modulate(x_s, g_mix[layer], sh_ms, sc_ms)
        i = layer // N_MIXERS
        if layer % N_MIXERS == 0:
            lambda_init = 0.8 - 0.6 * math.exp(-0.3 * layer)
            lam = (jnp.exp(jnp.sum(lam_q1_a[i] * lam_k1_a[i]).astype(jnp.float32))
                   - jnp.exp(jnp.sum(lam_q2_a[i] * lam_k2_a[i]).astype(jnp.float32))
                   + lambda_init)
            q_p, k_p, v_p = diff_attn_project(h_p, w_qkv_a[i], q_norm_a[i], k_norm_a[i])
            o_p = diff_attention(q_p, k_p, v_p, lam)
            new_a_k.append(k_p)
            new_a_v.append(v_p)
            q_s, k_s, v_s = diff_attn_project(h_s, w_qkv_a[i], q_norm_a[i], k_norm_a[i])
            q_s = apply_axial_rope(q_s, cos_a, sin_a)
            k_s = apply_axial_rope(k_s, cos_a, sin_a)
            k_all = jnp.concatenate([cache_a_k[:, i], k_s], axis=1)
            v_all = jnp.concatenate([cache_a_v[:, i], v_s], axis=1)
            o_s = diff_attention(q_s, k_all, v_all, lam)
            mix_p = diff_attn_merge(o_p, subln_a[i], lambda_init, w_o_a[i])
            mix_s = diff_attn_merge(o_s, subln_a[i], lambda_init, w_o_a[i])
        else:
            q_p, k_p, v_p = gqa_project(h_p, w_qkv_b[i], q_norm_b[i], k_norm_b[i])
            o_p = context_sink_attention(q_p, k_p, v_p, sinks_b[i])
            new_b_k.append(k_p)
            new_b_v.append(v_p)
            q_s, k_s, v_s = gqa_project(h_s, w_qkv_b[i], q_norm_b[i], k_norm_b[i])
            q_s = apply_axial_rope(q_s, cos_b, sin_b)
            k_s = apply_axial_rope(k_s, cos_b, sin_b)
            o_s = latent_window_sink_attention(q_s, k_s, v_s, cache_b_k[:, i], cache_b_v[:, i], sinks_b[i])
            mix_p = gqa_merge(o_p, w_o_b[i])
            mix_s = gqa_merge(o_s, w_o_b[i])
        x_p = x_p + gt_mp * mix_p
        x_s = x_s + gt_ms * mix_s
        f_p = moe_ffn(modulate(x_p, g_ffn[layer], sh_fp, sc_fp),
                      w_router[layer], b_router[layer], w_in[layer], b_in[layer], w_out[layer], b_out[layer])
        f_s = moe_ffn(modulate(x_s, g_ffn[layer], sh_fs, sc_fs),
                      w_router[layer], b_router[layer], w_in[layer], b_in[layer], w_out[layer], b_out[layer])
        x_p = x_p + gt_fp * f_p
        x_s = x_s + gt_fs * f_s

    state_a_k = jnp.stack(new_a_k, axis=1)
    state_a_v = jnp.stack(new_a_v, axis=1)
    state_b_k = jnp.stack(new_b_k, axis=1)
    state_b_v = jnp.stack(new_b_v, axis=1)
    return (x_p, x_s, state_a_k, state_a_v, state_b_k, state_b_v)
```

```python
import functools
import math

import jax
import jax.numpy as jnp
from jax import lax
from jax.experimental import pallas as pl
from jax.experimental.pallas import tpu as pltpu

F32 = jnp.float32
BF16 = jnp.bfloat16
I32 = jnp.int32
U32 = jnp.uint32

LANE = 128
HEAD = 64
NORM_EPS = 1e-6
ROPE_BASE = 10000.0
GRID_W = 64
WINDOW = 128
TOP_K = 4
SWIGLU_LIMIT = 7.0
SWIGLU_ALPHA = 1.702
NEG_INF = -1e30
LOG2E = 1.4426950408889634
N_COND = 8
VMEM_LIMIT = 56 * 1024 * 1024


def _params(n_axes):
    return pltpu.CompilerParams(
        dimension_semantics=("arbitrary",) * n_axes, vmem_limit_bytes=VMEM_LIMIT)


def _pick(n, prefs):
    for p in prefs:
        if n % p == 0:
            return p
    raise ValueError(f"no tile in {prefs} divides {n}")


def _dot(a, b):
    return jnp.dot(a, b, preferred_element_type=F32)


def _dot_nt(a, b):
    return lax.dot_general(a, b, (((1,), (1,)), ((), ())), preferred_element_type=F32)


def _split(v):
    hi = v.astype(BF16)
    lo = (v - hi.astype(F32)).astype(BF16)
    return hi, lo


def _dot3(a, b):
    a_hi, a_lo = _split(a)
    b_hi, b_lo = _split(b)
    return _dot(a_hi, b_hi) + _dot(a_lo, b_hi) + _dot(a_hi, b_lo)


def _ada_kernel(c_ref, w_ref, b_ref, o_ref):
    c = c_ref[...]
    s = c * jax.nn.sigmoid(c)
    o_ref[...] = _dot3(s, w_ref[...]) + b_ref[...]


def _ada_params(cond, w_ada, b_ada):
    depth, d, n = w_ada.shape
    tn = _pick(n, (512, 256, 128))
    return pl.pallas_call(
        _ada_kernel,
        out_shape=jax.ShapeDtypeStruct((depth, N_COND, n), F32),
        grid=(depth, n // tn),
        in_specs=[
            pl.BlockSpec((N_COND, d), lambda l, j: (0, 0)),
            pl.BlockSpec((None, d, tn), lambda l, j: (l, 0, j)),
            pl.BlockSpec((None, 1, tn), lambda l, j: (l, 0, j)),
        ],
        out_specs=pl.BlockSpec((None, N_COND, tn), lambda l, j: (l, 0, j)),
        compiler_params=_params(2),
        name="ada_params",
    )(cond, w_ada, b_ada.reshape(depth, 1, n))


def _modulate(x, g, sh, sc):
    xn = x * lax.rsqrt(jnp.mean(x * x, axis=-1, keepdims=True) + NORM_EPS)
    return xn * g * (1.0 + sc) + sh


def _qkv_kernel(x_ref, sh_ref, sc_ref, g_ref, w_ref, gain_ref, cos_ref, sin_ref, grp_ref,
                o_ref, h_scr, *, n_norm_tiles, tn):
    j = pl.program_id(1)

    @pl.when(j == 0)
    def _():
        h_scr[...] = _modulate(x_ref[...], g_ref[...], sh_ref[...], sc_ref[...]).astype(BF16)

    y = _dot(h_scr[...], w_ref[...])

    @pl.when(j < n_norm_tiles)
    def _():
        lane = lax.broadcasted_iota(I32, (1, LANE), 1)
        first = (lane & 16) == 0
        cos = cos_ref[...]
        sin = sin_ref[...]
        for s in range(tn // LANE):
            cols = slice(s * LANE, (s + 1) * LANE)
            ys = y[:, cols]
            ms = _dot((ys * ys).astype(BF16), grp_ref[...])
            yn = ys * lax.rsqrt(ms + NORM_EPS) * gain_ref[:, cols]
            rot = jnp.where(first, pltpu.roll(yn, LANE - 16, 1), pltpu.roll(yn, 16, 1))
            o_ref[:, cols] = yn * cos + rot * sin

    @pl.when(j >= n_norm_tiles)
    def _():
        o_ref[...] = y


def _cond_of_tile(i, n_ctx_tiles, tiles_per_seq):
    return jnp.where(i < n_ctx_tiles, 0, 1 + (i - n_ctx_tiles) // tiles_per_seq)


def _mod_spec(layer, which, d, cond_fn):
    return pl.BlockSpec((None, None, None, 1, d),
                        lambda i, *_: (layer, cond_fn(i), which, 0, 0))


def _qkv_project(x, mod, layer, g_mix, w, gain, cos_tab, sin_tab, grp, *, tm, n_norm, t_ctx, n_lat):
    t, d = x.shape
    n = w.shape[1]
    tn = _pick(math.gcd(n_norm, n - n_norm), (512, 256, 128))
    n_ctx_tiles = t_ctx // tm
    tiles_per_seq = n_lat // tm
    cond_fn = functools.partial(_cond_of_tile, n_ctx_tiles=n_ctx_tiles, tiles_per_seq=tiles_per_seq)

    def rope_idx(i, j):
        return (jnp.where(i < n_ctx_tiles, 0, 1 + (i - n_ctx_tiles) % tiles_per_seq), 0)

    return pl.pallas_call(
        functools.partial(_qkv_kernel, n_norm_tiles=n_norm // tn, tn=tn),
        out_shape=jax.ShapeDtypeStruct((t, n), F32),
        grid=(t // tm, n // tn),
        in_specs=[
            pl.BlockSpec((tm, d), lambda i, j: (i, 0)),
            _mod_spec(layer, 0, d, cond_fn),
            _mod_spec(layer, 1, d, cond_fn),
            pl.BlockSpec((1, d), lambda i, j: (0, 0)),
            pl.BlockSpec((d, tn), lambda i, j: (0, j)),
            pl.BlockSpec((1, tn), lambda i, j: (0, j)),
            pl.BlockSpec((tm, LANE), rope_idx),
            pl.BlockSpec((tm, LANE), rope_idx),
            pl.BlockSpec((LANE, LANE), lambda i, j: (0, 0)),
        ],
        out_specs=pl.BlockSpec((tm, tn), lambda i, j: (i, j)),
        scratch_shapes=[pltpu.VMEM((tm, d), BF16)],
        compiler_params=_params(2),
        name=f"qkv_project_{layer}",
    )(x, mod, mod, g_mix.reshape(1, d), w, gain, cos_tab, sin_tab, grp)


def _rope_tables(n_lat, tm):
    rows = n_lat // GRID_W
    row = jnp.broadcast_to(jnp.arange(rows, dtype=F32)[:, None], (rows, GRID_W)).reshape(-1)
    col = jnp.broadcast_to(jnp.arange(GRID_W, dtype=F32)[None, :], (rows, GRID_W)).reshape(-1)
    quarter = HEAD // 4
    inv = ROPE_BASE ** (-jnp.arange(quarter, dtype=F32) / quarter)
    ang_r = row[:, None] * inv[None, :]
    ang_c = col[:, None] * inv[None, :]
    ang = jnp.concatenate([ang_r, ang_r, ang_c, ang_c], axis=-1)
    sign = jnp.where((jnp.arange(HEAD) % 32) < 16, -1.0, 1.0).astype(F32)
    cos = jnp.cos(ang)
    sin = jnp.sin(ang) * sign
    cos = jnp.concatenate([cos, cos], axis=-1)
    sin = jnp.concatenate([sin, sin], axis=-1)
    cos = jnp.concatenate([jnp.ones((tm, LANE), F32), cos], axis=0)
    sin = jnp.concatenate([jnp.zeros((tm, LANE), F32), sin], axis=0)
    return cos, sin


def _attn_a_kernel(lamp_ref, sub_ref, q_ref, *rest, n_ctx, n_own, tq, tk, heads, lambda_init):
    if n_ctx:
        ck_ref, cv_ref, k_ref, v_ref, o_ref, kb, vb, acc = rest
    else:
        k_ref, v_ref, o_ref, kb, vb, acc = rest
    qi = pl.program_id(2)
    n_keys = n_own + n_ctx

    @pl.when(qi == 0)
    def _():
        kb[0:n_own, :] = k_ref[...].astype(BF16)
        vb[0:n_own, :] = v_ref[...].astype(BF16)
        if n_ctx:
            kb[n_own:n_keys, :] = ck_ref[...].astype(BF16)
            vb[n_own:n_keys, :] = cv_ref[...].astype(BF16)

    lp = lamp_ref[...]
    lam = (jnp.exp(jnp.sum(lp[0:1] * lp[1:2], axis=-1, keepdims=True))
           - jnp.exp(jnp.sum(lp[2:3] * lp[3:4], axis=-1, keepdims=True)) + lambda_init)
    lo = lax.broadcasted_iota(I32, (1, LANE), 1) < HEAD
    n_main = n_own // tk
    tail = n_keys - n_main * tk

    for h in range(heads):
        cols = slice(h * LANE, (h + 1) * LANE)
        q = q_ref[:, cols] * LOG2E
        q2 = jnp.concatenate([jnp.where(lo, q, 0.0), jnp.where(lo, 0.0, q)], axis=0).astype(BF16)
        acc[...] = jnp.zeros_like(acc)

        def step(k_t, v_t, carry):
            m, l = carry
            s = _dot_nt(q2, k_t)
            m_new = jnp.maximum(m, jnp.max(s, axis=-1, keepdims=True))
            alpha = jnp.exp2(m - m_new)
            p = jnp.exp2(s - m_new)
            acc[...] = alpha * acc[...] + _dot(p.astype(BF16), v_t)
            return m_new, alpha * l + jnp.sum(p, axis=-1, keepdims=True)

        def body(j, carry):
            st = pl.multiple_of(j * tk, tk)
            return step(kb[pl.ds(st, tk), cols], vb[pl.ds(st, tk), cols], carry)

        carry = (jnp.full((2 * tq, 1), NEG_INF, F32), jnp.zeros((2 * tq, 1), F32))
        if n_main:
            carry = lax.fori_loop(0, n_main, body, carry)
        if tail:
            carry = step(kb[n_main * tk:n_keys, cols], vb[n_main * tk:n_keys, cols], carry)
        _, l = carry
        on = acc[...] / l
        o = on[0:tq] - lam * on[tq:2 * tq]
        o = o * lax.rsqrt(jnp.mean(o * o, axis=-1, keepdims=True) + NORM_EPS)
        o_ref[:, cols] = (o * sub_ref[...] * (1.0 - lambda_init)).astype(BF16)


def _diff_attention(qkv, lamp, sub_gain, cache_k, cache_v, *, row0, n_batch, n_own, heads_total,
                    lambda_init):
    width = heads_total * LANE
    heads = 2 if heads_total % 2 == 0 else 1
    gw = heads * LANE
    n_groups = heads_total // heads
    tq = _pick(n_own, (256, 128))
    tk = _pick(n_own, (512, 256, 128))
    n_ctx = 0 if cache_k is None else cache_k.shape[0] // n_batch
    q_tiles = n_own // tq
    q_row0 = row0 // tq
    kv_row0 = row0 // n_own
    sec = width // gw

    in_specs = [
        pl.BlockSpec((4, HEAD), lambda b, g, i: (0, 0)),
        pl.BlockSpec((1, LANE), lambda b, g, i: (0, 0)),
        pl.BlockSpec((tq, gw), lambda b, g, i: (q_row0 + b * q_tiles + i, g)),
    ]
    args = [lamp, sub_gain.reshape(1, LANE), qkv]
    if n_ctx:
        in_specs += [pl.BlockSpec((n_ctx, gw), lambda b, g, i: (b, g)),
                     pl.BlockSpec((n_ctx, gw), lambda b, g, i: (b, g))]
        args += [cache_k, cache_v]
    in_specs += [pl.BlockSpec((n_own, gw), lambda b, g, i: (kv_row0 + b, sec + g)),
                 pl.BlockSpec((n_own, gw), lambda b, g, i: (kv_row0 + b, 2 * sec + g))]
    args += [qkv, qkv]
    n_keys = n_own + n_ctx
    return pl.pallas_call(
        functools.partial(_attn_a_kernel, n_ctx=n_ctx, n_own=n_own, tq=tq, tk=tk, heads=heads,
                          lambda_init=lambda_init),
        out_shape=jax.ShapeDtypeStruct((n_batch * n_own, width), BF16),
        grid=(n_batch, n_groups, q_tiles),
        in_specs=in_specs,
        out_specs=pl.BlockSpec((tq, gw), lambda b, g, i: (b * q_tiles + i, g)),
        scratch_shapes=[pltpu.VMEM((n_keys, gw), BF16), pltpu.VMEM((n_keys, gw), BF16),
                        pltpu.VMEM((2 * tq, LANE), F32)],
        compiler_params=_params(3),
        name="diff_attention_ctx" if n_ctx else "diff_attention_self",
    )(*args)


def _attn_b_kernel(sink_ref, q_ref, *rest, n_ctx, n_own, tq, windowed, slabs):
    if n_ctx:
        ck_ref, cv_ref, k_ref, v_ref, o_ref, kb, vb = rest
    else:
        k_ref, v_ref, o_ref, kb, vb = rest
    g = pl.program_id(1)
    qi = pl.program_id(2)
    pad = WINDOW if windowed else 0
    own0 = n_ctx + pad

    @pl.when(qi == 0)
    def _():
        if n_ctx:
            kb[0:n_ctx, :] = ck_ref[...].astype(BF16)
            vb[0:n_ctx, :] = cv_ref[...].astype(BF16)
        if pad:
            zeros = jnp.zeros((pad, LANE), BF16)
            for ref in (kb, vb):
                ref[n_ctx:own0, :] = zeros
                ref[own0 + n_own:own0 + n_own + pad, :] = zeros
        kb[own0:own0 + n_own, :] = k_ref[...].astype(BF16)
        vb[own0:own0 + n_own, :] = v_ref[...].astype(BF16)

    if windowed:
        wlen = tq + 2 * pad
        st = pl.multiple_of(n_ctx + qi * tq, tq)
        k_w = kb[pl.ds(st, wlen), :]
        v_w = vb[pl.ds(st, wlen), :]
        r = lax.broadcasted_iota(I32, (2 * tq, wlen), 0) & (tq - 1)
        c = lax.broadcasted_iota(I32, (2 * tq, wlen), 1)
        kpos = qi * tq - pad + c
        d = c - r
        mask = (d >= 0) & (d <= 2 * pad) & (kpos >= 0) & (kpos < n_own)
    else:
        k_w = kb[own0:own0 + n_own, :]
        v_w = vb[own0:own0 + n_own, :]
        mask = None
    if n_ctx:
        k_c = kb[0:n_ctx, :]
        v_c = vb[0:n_ctx, :]

    lo = lax.broadcasted_iota(I32, (1, LANE), 1) < HEAD
    top = lax.broadcasted_iota(I32, (2 * tq, 1), 0) < tq
    for t in range(slabs):
        cols = slice(t * LANE, (t + 1) * LANE)
        q = q_ref[:, cols] * LOG2E
        q2 = jnp.concatenate([jnp.where(lo, q, 0.0), jnp.where(lo, 0.0, q)], axis=0).astype(BF16)
        head0 = g * (2 * slabs) + 2 * t
        sk = jnp.where(top, sink_ref[head0], sink_ref[head0 + 1]) * LOG2E
        s_w = _dot_nt(q2, k_w)
        if mask is not None:
            s_w = jnp.where(mask, s_w, NEG_INF)
        m = jnp.maximum(jnp.max(s_w, axis=-1, keepdims=True), sk)
        if n_ctx:
            s_c = _dot_nt(q2, k_c)
            m = jnp.maximum(m, jnp.max(s_c, axis=-1, keepdims=True))
        p_w = jnp.exp2(s_w - m)
        den = jnp.sum(p_w, axis=-1, keepdims=True) + jnp.exp2(sk - m)
        o = _dot(p_w.astype(BF16), v_w)
        if n_ctx:
            p_c = jnp.exp2(s_c - m)
            den = den + jnp.sum(p_c, axis=-1, keepdims=True)
            o = o + _dot(p_c.astype(BF16), v_c)
        o = o / den
        o_ref[:, cols] = jnp.where(lo, o[0:tq], o[tq:2 * tq]).astype(BF16)


def _sink_attention(qkv, sinks, cache_k, cache_v, *, row0, n_batch, n_own, q_width, kv_heads,
                    windowed):
    gw = q_width // kv_heads
    slabs = gw // LANE
    tq = _pick(n_own, (256, 128))
    n_ctx = 0 if cache_k is None else cache_k.shape[0] // n_batch
    q_tiles = n_own // tq
    q_row0 = row0 // tq
    kv_row0 = row0 // n_own
    k_col0 = q_width // LANE
    v_col0 = k_col0 + kv_heads
    pad = WINDOW if windowed else 0

    in_specs = [
        pl.BlockSpec(memory_space=pltpu.SMEM),
        pl.BlockSpec((tq, gw), lambda b, g, i: (q_row0 + b * q_tiles + i, g)),
    ]
    args = [sinks, qkv]
    if n_ctx:
        in_specs += [pl.BlockSpec((n_ctx, LANE), lambda b, g, i: (b, g)),
                     pl.BlockSpec((n_ctx, LANE), lambda b, g, i: (b, g))]
        args += [cache_k, cache_v]
    in_specs += [pl.BlockSpec((n_own, LANE), lambda b, g, i: (kv_row0 + b, k_col0 + g)),
                 pl.BlockSpec((n_own, LANE), lambda b, g, i: (kv_row0 + b, v_col0 + g))]
    args += [qkv, qkv]
    n_rows = n_ctx + n_own + 2 * pad
    return pl.pallas_call(
        functools.partial(_attn_b_kernel, n_ctx=n_ctx, n_own=n_own, tq=tq, windowed=windowed,
                          slabs=slabs),
        out_shape=jax.ShapeDtypeStruct((n_batch * n_own, q_width), BF16),
        grid=(n_batch, kv_heads, q_tiles),
        in_specs=in_specs,
        out_specs=pl.BlockSpec((tq, gw), lambda b, g, i: (b * q_tiles + i, g)),
        scratch_shapes=[pltpu.VMEM((n_rows, LANE), BF16), pltpu.VMEM((n_rows, LANE), BF16)],
        compiler_params=_params(3),
        name="sink_attention_window" if windowed else "sink_attention_self",
    )(*args)


def _out_kernel(o_ref, wo_ref, x_ref, gt_ref, g_ref, sh_ref, sc_ref, wr_ref, br_ref,
                x1_ref, hp_ref, route_ref, gate_ref, cnt_ref, carry, *, n_exp, tm):
    i = pl.program_id(0)

    @pl.when(i == 0)
    def _():
        carry[...] = jnp.zeros_like(carry)

    x1 = x_ref[...] + gt_ref[...] * _dot(o_ref[...], wo_ref[...])
    x1_ref[...] = x1
    h = _modulate(x1, g_ref[...], sh_ref[...], sc_ref[...])
    h_hi, h_lo = _split(h)
    half = h.shape[1] // 2
    ua = pltpu.bitcast(h_hi[:, :half].astype(F32), U32)
    ub = pltpu.bitcast(h_hi[:, half:].astype(F32), U32)
    hp_ref[...] = ua | (ub >> 16)

    w_hi, w_lo = _split(wr_ref[...])
    logits = _dot(h_hi, w_hi) + _dot(h_lo, w_hi) + _dot(h_hi, w_lo) + br_ref[...]

    eio = lax.broadcasted_iota(I32, (tm, n_exp), 1).astype(F32)
    work = logits
    tops, idxs, hots = [], [], []
    for _ in range(TOP_K):
        mk = jnp.max(work, axis=-1, keepdims=True)
        ik = jnp.min(jnp.where(work == mk, eio, float(n_exp)), axis=-1, keepdims=True)
        hot = eio == ik
        tops.append(mk)
        idxs.append(ik)
        hots.append(hot)
        work = jnp.where(hot, -3e38, work)
    es = [jnp.exp(t - tops[0]) for t in tops]
    den = es[0] + es[1] + es[2] + es[3]

    hot_all = jnp.zeros((tm, n_exp), F32)
    for hot in hots:
        hot_all = hot_all + jnp.where(hot, 1.0, 0.0)
    tri = jnp.where(lax.broadcasted_iota(I32, (tm, tm), 0) > lax.broadcasted_iota(I32, (tm, tm), 1),
                    1.0, 0.0).astype(BF16)
    base = _dot(tri, hot_all.astype(BF16)) + carry[...]
    ranks = [jnp.sum(jnp.where(hot, base, 0.0), axis=-1, keepdims=True).astype(I32) for hot in hots]
    carry[...] = carry[...] + jnp.sum(hot_all, axis=0, keepdims=True)
    cnt_ref[...] = carry[...]

    col = lax.broadcasted_iota(I32, (tm, LANE), 1)
    route = jnp.zeros((tm, LANE), I32)
    gate = jnp.zeros((tm, LANE), F32)
    for k in range(TOP_K):
        route = jnp.where(col == k, idxs[k].astype(I32), route)
        route = jnp.where(col == TOP_K + k, ranks[k], route)
        gate = jnp.where(col == k, es[k] / den, gate)
    route_ref[...] = route
    gate_ref[...] = gate


def _out_project(o, w_o, x, mod, layer, g_ffn, w_router, b_router, *, t_ctx, n_lat):
    t, d = x.shape
    kdim = o.shape[1]
    n_exp = w_router.shape[1]
    tm = _pick(math.gcd(t_ctx, n_lat), (512, 256, 128))
    cond_fn = functools.partial(_cond_of_tile, n_ctx_tiles=t_ctx // tm, tiles_per_seq=n_lat // tm)
    row = lambda i: (i, 0)
    fixed = lambda i: (0, 0)
    return pl.pallas_call(
        functools.partial(_out_kernel, n_exp=n_exp, tm=tm),
        out_shape=(jax.ShapeDtypeStruct((t, d), F32), jax.ShapeDtypeStruct((t, d // 2), U32),
                   jax.ShapeDtypeStruct((t, LANE), I32), jax.ShapeDtypeStruct((t, LANE), F32),
                   jax.ShapeDtypeStruct((1, n_exp), F32)),
        grid=(t // tm,),
        in_specs=[
            pl.BlockSpec((tm, kdim), row),
            pl.BlockSpec((kdim, d), fixed),
            pl.BlockSpec((tm, d), row),
            _mod_spec(layer, 2, d, cond_fn),
            pl.BlockSpec((1, d), fixed),
            _mod_spec(layer, 3, d, cond_fn),
            _mod_spec(layer, 4, d, cond_fn),
            pl.BlockSpec((d, n_exp), fixed),
            pl.BlockSpec((1, n_exp), fixed),
        ],
        out_specs=(pl.BlockSpec((tm, d), row), pl.BlockSpec((tm, d // 2), row),
                   pl.BlockSpec((tm, LANE), row), pl.BlockSpec((tm, LANE), row),
                   pl.BlockSpec((1, n_exp), fixed)),
        scratch_shapes=[pltpu.VMEM((1, n_exp), F32)],
        compiler_params=_params(1),
        name=f"out_project_route_{layer}",
    )(o, w_o, x, mod, g_ffn.reshape(1, d), mod, mod, w_router, b_router.reshape(1, n_exp))


def _gather_kernel(nv_ref, tok_hbm, h_hbm, o_ref, idx, isem, sem, *, tg):
    i = pl.program_id(0)

    @pl.when(i < nv_ref[0])
    def _():
        cp = pltpu.make_async_copy(tok_hbm.at[i], idx, isem)
        cp.start()
        cp.wait()

        def row_copy(r):
            return pltpu.make_async_copy(h_hbm.at[pl.ds(idx[0, r], 1)], o_ref.at[pl.ds(r, 1)], sem)

        def issue(r, c):
            row_copy(r).start()
            return c

        def drain(r, c):
            row_copy(r).wait()
            return c

        lax.fori_loop(0, tg, issue, 0)
        lax.fori_loop(0, tg, drain, 0)

    @pl.when(i >= nv_ref[0])
    def _():
        o_ref[...] = jnp.zeros_like(o_ref)


def _gather_rows(hp, slot_token, n_valid, *, tg):
    cap = slot_token.shape[0]
    dh = hp.shape[1]
    grid_spec = pltpu.PrefetchScalarGridSpec(
        num_scalar_prefetch=1,
        grid=(cap // tg,),
        in_specs=[pl.BlockSpec(memory_space=pl.ANY), pl.BlockSpec(memory_space=pl.ANY)],
        out_specs=pl.BlockSpec((tg, dh), lambda i, nv: (i, 0)),
        scratch_shapes=[pltpu.SMEM((1, tg), I32), pltpu.SemaphoreType.DMA, pltpu.SemaphoreType.DMA],
    )
    return pl.pallas_call(
        functools.partial(_gather_kernel, tg=tg),
        out_shape=jax.ShapeDtypeStruct((cap, dh), hp.dtype),
        grid_spec=grid_spec,
        compiler_params=_params(1),
        name="moe_dispatch_gather",
    )(n_valid, slot_token.reshape(cap // tg, 1, tg), hp)


def _moe_kernel(te_ref, nv_ref, xs_ref, wg_ref, wu_ref, bg_ref, bu_ref, wo_ref, bo_ref, y_ref, xb):
    i = pl.program_id(0)
    f = pl.program_id(1)

    @pl.when(i < nv_ref[0])
    def _():
        half = xs_ref.shape[1]

        @pl.when(f == 0)
        def _():
            u = xs_ref[...]
            xb[:, :half] = pltpu.bitcast(u & jnp.uint32(0xFFFF0000), F32).astype(BF16)
            xb[:, half:] = pltpu.bitcast(u << 16, F32).astype(BF16)

        x = xb[...]
        g = _dot(x, wg_ref[...].astype(BF16)) + bg_ref[...]
        u = _dot(x, wu_ref[...].astype(BF16)) + bu_ref[...]
        g = jnp.minimum(g, SWIGLU_LIMIT)
        u = jnp.clip(u, -SWIGLU_LIMIT, SWIGLU_LIMIT)
        act = g * jax.nn.sigmoid(SWIGLU_ALPHA * g) * (u + 1.0)
        part = _dot(act.astype(BF16), wo_ref[...].astype(BF16))

        @pl.when(f == 0)
        def _():
            y_ref[...] = part + bo_ref[...]

        @pl.when(f > 0)
        def _():
            y_ref[...] = y_ref[...] + part

    @pl.when(jnp.logical_and(i >= nv_ref[0], f == 0))
    def _():
        y_ref[...] = jnp.zeros_like(y_ref)


def _moe_experts(xs, tile_expert, n_valid, layer, w_in, b_in, w_out, b_out, *, tm):
    cap, half = xs.shape
    depth, n_exp, d, two_ff = w_in.shape
    d_ff = two_ff // 2
    tf = _pick(d_ff, (256, 128))
    nf = d_ff // tf

    def tile_idx(i, nv):
        return jnp.minimum(i, nv[0] - 1)

    def chunk_idx(i, f, nv):
        return jnp.where(i < nv[0], f, nf - 1)

    wblk = (None, None, d, tf)
    bblk = (None, None, 1, tf)
    grid_spec = pltpu.PrefetchScalarGridSpec(
        num_scalar_prefetch=2,
        grid=(cap // tm, nf),
        in_specs=[
            pl.BlockSpec((tm, half), lambda i, f, te, nv: (tile_idx(i, nv), 0)),
            pl.BlockSpec(wblk, lambda i, f, te, nv: (layer, te[i], 0, chunk_idx(i, f, nv))),
            pl.BlockSpec(wblk, lambda i, f, te, nv: (layer, te[i], 0, nf + chunk_idx(i, f, nv))),
            pl.BlockSpec(bblk, lambda i, f, te, nv: (layer, te[i], 0, chunk_idx(i, f, nv))),
            pl.BlockSpec(bblk, lambda i, f, te, nv: (layer, te[i], 0, nf + chunk_idx(i, f, nv))),
            pl.BlockSpec((None, None, tf, d),
                         lambda i, f, te, nv: (layer, te[i], chunk_idx(i, f, nv), 0)),
            pl.BlockSpec((None, None, 1, d), lambda i, f, te, nv: (layer, te[i], 0, 0)),
        ],
        out_specs=pl.BlockSpec((tm, d), lambda i, f, te, nv: (i, 0)),
        scratch_shapes=[pltpu.VMEM((tm, d), BF16)],
    )
    b_in4 = b_in.reshape(depth, n_exp, 1, two_ff)
    return pl.pallas_call(
        _moe_kernel,
        out_shape=jax.ShapeDtypeStruct((cap, d), F32),
        grid_spec=grid_spec,
        compiler_params=_params(2),
        name=f"moe_experts_{layer}",
    )(tile_expert, n_valid, xs, w_in, w_in, b_in4, b_in4, w_out, b_out.reshape(depth, n_exp, 1, d))


def _combine_kernel(slot_hbm, gate_ref, x_ref, gt_ref, y_hbm, o_ref, idx, buf, isem, sem, *, tc):
    i = pl.program_id(0)
    cp = pltpu.make_async_copy(slot_hbm.at[i], idx, isem)
    cp.start()
    cp.wait()

    def row_copy(r, k):
        return pltpu.make_async_copy(y_hbm.at[pl.ds(idx[0, r * TOP_K + k], 1)],
                                     buf.at[k, pl.ds(r, 1)], sem)

    def issue(r, c):
        for k in range(TOP_K):
            row_copy(r, k).start()
        return c

    def drain(r, c):
        for k in range(TOP_K):
            row_copy(r, k).wait()
        return c

    lax.fori_loop(0, tc, issue, 0)
    lax.fori_loop(0, tc, drain, 0)
    gate = gate_ref[...]
    f = gate[:, 0:1] * buf[0]
    for k in range(1, TOP_K):
        f = f + gate[:, k:k + 1] * buf[k]
    o_ref[...] = x_ref[...] + gt_ref[...] * f


def _moe_combine(slot, gates, x1, mod, layer, y_sorted, *, t_ctx, n_lat):
    t, d = x1.shape
    tc = _pick(math.gcd(t_ctx, n_lat), (128,))
    cond_fn = functools.partial(_cond_of_tile, n_ctx_tiles=t_ctx // tc, tiles_per_seq=n_lat // tc)
    row = lambda i: (i, 0)
    return pl.pallas_call(
        functools.partial(_combine_kernel, tc=tc),
        out_shape=jax.ShapeDtypeStruct((t, d), F32),
        grid=(t // tc,),
        in_specs=[
            pl.BlockSpec(memory_space=pl.ANY),
            pl.BlockSpec((tc, LANE), row),
            pl.BlockSpec((tc, d), row),
            _mod_spec(layer, 5, d, cond_fn),
            pl.BlockSpec(memory_space=pl.ANY),
        ],
        out_specs=pl.BlockSpec((tc, d), row),
        scratch_shapes=[pltpu.SMEM((1, tc * TOP_K), I32), pltpu.VMEM((TOP_K, tc, d), F32),
                        pltpu.SemaphoreType.DMA, pltpu.SemaphoreType.DMA],
        compiler_params=_params(1),
        name="moe_combine",
    )(slot.reshape(t // tc, 1, tc * TOP_K), gates, x1, mod, y_sorted)


def _moe_tile(n_assign, n_exp):
    per_expert = n_assign // n_exp
    for tm in (1024, 512, 256):
        if per_expert >= 3 * tm:
            return tm
    return 128


def _moe_ffn(hp, route, gates, counts, x1, mod, layer, w_in, b_in, w_out, b_out, *, t_ctx, n_lat):
    t = x1.shape[0]
    n_exp = w_in.shape[1]
    n_assign = t * TOP_K
    tm = _moe_tile(n_assign, n_exp)
    cap = -(-(n_assign + n_exp * (tm - 1)) // tm) * tm
    n_tiles = cap // tm

    counts = counts.reshape(n_exp).astype(I32)
    padded = (counts + tm - 1) // tm * tm
    pend = jnp.cumsum(padded)
    pstart = pend - padded
    idx = route[:, :TOP_K]
    slot = pstart[idx] + route[:, TOP_K:2 * TOP_K]
    token_of = jnp.broadcast_to(jnp.arange(t, dtype=I32)[:, None], (t, TOP_K))
    slot_token = jnp.zeros((cap,), I32).at[slot.reshape(-1)].set(token_of.reshape(-1))
    n_valid = (pend[-1] // tm).astype(I32)
    tile_start = jnp.minimum(jnp.arange(n_tiles, dtype=I32), n_valid - 1) * tm
    tile_expert = jnp.minimum(jnp.searchsorted(pend, tile_start, side='right'),
                              n_exp - 1).astype(I32)

    tg = tm
    xs = _gather_rows(hp, slot_token, (n_valid * (tm // tg)).reshape(1), tg=tg)
    y_sorted = _moe_experts(xs, tile_expert, n_valid.reshape(1), layer, w_in, b_in, w_out, b_out,
                            tm=tm)
    return _moe_combine(slot, gates, x1, mod, layer, y_sorted, t_ctx=t_ctx, n_lat=n_lat)


def _head_table(gain, n_heads, scale=1.0):
    return jnp.tile(gain * scale, n_heads)


def kernel(x_prompt, x_sample, cache_a_k, cache_a_v, cache_b_k, cache_b_v, c, c_ctx, w_ada, b_ada, g_mix, g_ffn, w_qkv_a, q_norm_a, k_norm_a, lam_q1_a, lam_k1_a, lam_q2_a, lam_k2_a, subln_a, w_o_a, w_qkv_b, q_norm_b, k_norm_b, sinks_b, w_o_b, w_router, b_router, w_in, b_in, w_out, b_out):
    batch, seq, d = x_prompt.shape
    dec_batch, n_lat, _ = x_sample.shape
    depth = w_ada.shape[0]
    past = cache_a_k.shape[2]
    heads_a = cache_a_k.shape[3]
    kv_heads_b = cache_b_k.shape[3]
    q_heads_b = sinks_b.shape[1]
    t_ctx = batch * seq
    t_lat = dec_batch * n_lat
    assert dec_batch + 1 <= N_COND

    x = jnp.concatenate([x_prompt.reshape(t_ctx, d), x_sample.reshape(t_lat, d)], axis=0)
    cond = jnp.concatenate([c_ctx[None, :], c, jnp.zeros((N_COND - 1 - dec_batch, d), F32)], axis=0)
    mod = _ada_params(cond, w_ada, b_ada).reshape(depth, N_COND, 6, 1, d)

    tm_qkv = _pick(math.gcd(t_ctx, n_lat), (1024, 512, 256, 128))
    cos_tab, sin_tab = _rope_tables(n_lat, tm_qkv)
    grp = jnp.where((jnp.arange(LANE)[:, None] // HEAD) == (jnp.arange(LANE)[None, :] // HEAD),
                    1.0 / HEAD, 0.0).astype(BF16)

    wa = heads_a * LANE
    qw_b = q_heads_b * HEAD
    kvw_b = kv_heads_b * HEAD
    states = {"a_k": [], "a_v": [], "b_k": [], "b_v": []}

    for layer in range(depth):
        i = layer // 2
        if layer % 2 == 0:
            lambda_init = 0.8 - 0.6 * math.exp(-0.3 * layer)
            gain = jnp.concatenate([
                _head_table(q_norm_a[i], 2 * heads_a, HEAD ** -0.5),
                _head_table(k_norm_a[i], 2 * heads_a),
                jnp.ones((wa,), F32)]).reshape(1, 3 * wa)
            qkv = _qkv_project(x, mod, layer, g_mix[layer], w_qkv_a[i].astype(BF16), gain,
                               cos_tab, sin_tab, grp, tm=tm_qkv, n_norm=2 * wa, t_ctx=t_ctx, n_lat=n_lat)
            states["a_k"].append(qkv[:t_ctx, wa:2 * wa].reshape(batch, seq, heads_a, 2, HEAD))
            states["a_v"].append(qkv[:t_ctx, 2 * wa:].reshape(batch, seq, heads_a, LANE))
            lamp = jnp.stack([lam_q1_a[i], lam_k1_a[i], lam_q2_a[i], lam_k2_a[i]])
            o_p = _diff_attention(qkv, lamp, subln_a[i], None, None, row0=0, n_batch=batch,
                                  n_own=seq, heads_total=heads_a, lambda_init=lambda_init)
            o_s = _diff_attention(qkv, lamp, subln_a[i],
                                  cache_a_k[:, i].reshape(dec_batch * past, wa),
                                  cache_a_v[:, i].reshape(dec_batch * past, wa),
                                  row0=t_ctx, n_batch=dec_batch, n_own=n_lat, heads_total=heads_a,
                                  lambda_init=lambda_init)
            w_o = w_o_a[i]
        else:
            wq, wk, wv = jnp.split(w_qkv_b[i], [qw_b, qw_b + kvw_b], axis=1)
            dup = lambda w: jnp.concatenate([w.reshape(d, kv_heads_b, 1, HEAD)] * 2, axis=2).reshape(
                d, 2 * kvw_b)
            w_b = jnp.concatenate([wq, dup(wk), dup(wv)], axis=1).astype(BF16)
            gain = jnp.concatenate([
                _head_table(q_norm_b[i], q_heads_b, HEAD ** -0.5),
                _head_table(k_norm_b[i], 2 * kv_heads_b),
                jnp.ones((2 * kvw_b,), F32)]).reshape(1, qw_b + 4 * kvw_b)
            qkv = _qkv_project(x, mod, layer, g_mix[layer], w_b, gain, cos_tab, sin_tab, grp,
                               tm=tm_qkv, n_norm=qw_b + 2 * kvw_b, t_ctx=t_ctx, n_lat=n_lat)
            k_state = qkv[:t_ctx, qw_b:qw_b + 2 * kvw_b].reshape(batch, seq, kv_heads_b, 2, HEAD)
            v_state = qkv[:t_ctx, qw_b + 2 * kvw_b:].reshape(batch, seq, kv_heads_b, 2, HEAD)
            states["b_k"].append(k_state[:, :, :, 0])
            states["b_v"].append(v_state[:, :, :, 0])
            dup_c = lambda a: jnp.concatenate([a, a], axis=-1).reshape(dec_batch * past,
                                                                       2 * kvw_b)
            o_p = _sink_attention(qkv, sinks_b[i], None, None, row0=0, n_batch=batch, n_own=seq,
                                  q_width=qw_b, kv_heads=kv_heads_b, windowed=False)
            o_s = _sink_attention(qkv, sinks_b[i], dup_c(cache_b_k[:, i]), dup_c(cache_b_v[:, i]),
                                  row0=t_ctx, n_batch=dec_batch, n_own=n_lat, q_width=qw_b,
                                  kv_heads=kv_heads_b, windowed=True)
            w_o = w_o_b[i]

        o = jnp.concatenate([o_p, o_s], axis=0)
        x1, hp, route, gates, counts = _out_project(
            o, w_o.astype(BF16), x, mod, layer, g_ffn[layer], w_router[layer], b_router[layer],
            t_ctx=t_ctx, n_lat=n_lat)
        x = _moe_ffn(hp, route, gates, counts, x1, mod, layer, w_in, b_in, w_out, b_out,
                     t_ctx=t_ctx, n_lat=n_lat)

    return (x[:t_ctx].reshape(batch, seq, d), x[t_ctx:].reshape(dec_batch, n_lat, d),
            jnp.stack(states["a_k"], axis=1), jnp.stack(states["a_v"], axis=1),
            jnp.stack(states["b_k"], axis=1), jnp.stack(states["b_v"], axis=1))
```

```python
import functools
import math

import jax
import jax.numpy as jnp
from jax import lax
from jax.experimental import pallas as pl
from jax.experimental.pallas import tpu as pltpu

F32 = jnp.float32
BF16 = jnp.bfloat16
I32 = jnp.int32
U32 = jnp.uint32

LANE = 128
HEAD = 64
NORM_EPS = 1e-6
ROPE_BASE = 10000.0
GRID_W = 64
WINDOW = 128
TOP_K = 4
SWIGLU_LIMIT = 7.0
SWIGLU_ALPHA = 1.702
NEG_INF = -1e30
LOG2E = 1.4426950408889634
N_COND = 8
VMEM_LIMIT = 56 * 1024 * 1024


def _params(n_axes):
    return pltpu.CompilerParams(
        dimension_semantics=("arbitrary",) * n_axes, vmem_limit_bytes=VMEM_LIMIT)


def _pick(n, prefs):
    for p in prefs:
        if n % p == 0:
            return p
    raise ValueError(f"no tile in {prefs} divides {n}")


def _dot(a, b):
    return jnp.dot(a, b, preferred_element_type=F32)


def _dot_nt(a, b):
    return lax.dot_general(a, b, (((1,), (1,)), ((), ())), preferred_element_type=F32)


def _split(v):
    hi = v.astype(BF16)
    lo = (v - hi.astype(F32)).astype(BF16)
    return hi, lo


def _dot3(a, b):
    a_hi, a_lo = _split(a)
    b_hi, b_lo = _split(b)
    return _dot(a_hi, b_hi) + _dot(a_lo, b_hi) + _dot(a_hi, b_lo)


def _ada_kernel(c_ref, w_ref, b_ref, o_ref):
    c = c_ref[...]
    s = c * jax.nn.sigmoid(c)
    o_ref[...] = _dot3(s, w_ref[...]) + b_ref[...]


def _ada_params(cond, w_ada, b_ada):
    depth, d, n = w_ada.shape
    tn = _pick(n, (512, 256, 128))
    return pl.pallas_call(
        _ada_kernel,
        out_shape=jax.ShapeDtypeStruct((depth, N_COND, n), F32),
        grid=(depth, n // tn),
        in_specs=[
            pl.BlockSpec((N_COND, d), lambda l, j: (0, 0)),
            pl.BlockSpec((None, d, tn), lambda l, j: (l, 0, j)),
            pl.BlockSpec((None, 1, tn), lambda l, j: (l, 0, j)),
        ],
        out_specs=pl.BlockSpec((None, N_COND, tn), lambda l, j: (l, 0, j)),
        compiler_params=_params(2),
        name="ada_params",
    )(cond, w_ada, b_ada.reshape(depth, 1, n))


def _modulate(x, g, sh, sc):
    xn = x * lax.rsqrt(jnp.mean(x * x, axis=-1, keepdims=True) + NORM_EPS)
    return xn * g * (1.0 + sc) + sh


def _qkv_kernel(x_ref, sh_ref, sc_ref, g_ref, w_ref, gain_ref, cos_ref, sin_ref, grp_ref,
                o_ref, h_scr, *, n_norm_tiles, tn):
    j = pl.program_id(1)

    @pl.when(j == 0)
    def _():
        h_scr[...] = _modulate(x_ref[...], g_ref[...], sh_ref[...], sc_ref[...]).astype(BF16)

    y = _dot(h_scr[...], w_ref[...])

    @pl.when(j < n_norm_tiles)
    def _():
        lane = lax.broadcasted_iota(I32, (1, LANE), 1)
        first = (lane & 16) == 0
        cos = cos_ref[...]
        sin = sin_ref[...]
        for s in range(tn // LANE):
            cols = slice(s * LANE, (s + 1) * LANE)
            ys = y[:, cols]
            ms = _dot((ys * ys).astype(BF16), grp_ref[...])
            yn = ys * lax.rsqrt(ms + NORM_EPS) * gain_ref[:, cols]
            rot = jnp.where(first, pltpu.roll(yn, LANE - 16, 1), pltpu.roll(yn, 16, 1))
            o_ref[:, cols] = yn * cos + rot * sin

    @pl.when(j >= n_norm_tiles)
    def _():
        o_ref[...] = y


def _cond_of_tile(i, n_ctx_tiles, tiles_per_seq):
    return jnp.where(i < n_ctx_tiles, 0, 1 + (i - n_ctx_tiles) // tiles_per_seq)


def _mod_spec(layer, which, d, cond_fn):
    return pl.BlockSpec((None, None, None, 1, d),
                        lambda i, *_: (layer, cond_fn(i), which, 0, 0))


def _qkv_project(x, mod, layer, g_mix, w, gain, cos_tab, sin_tab, grp, *, tm, n_norm, t_ctx, n_lat):
    t, d = x.shape
    n = w.shape[1]
    tn = _pick(math.gcd(n_norm, n - n_norm), (512, 256, 128))
    n_ctx_tiles = t_ctx // tm
    tiles_per_seq = n_lat // tm
    cond_fn = functools.partial(_cond_of_tile, n_ctx_tiles=n_ctx_tiles, tiles_per_seq=tiles_per_seq)

    def rope_idx(i, j):
        return (jnp.where(i < n_ctx_tiles, 0, 1 + (i - n_ctx_tiles) % tiles_per_seq), 0)

    return pl.pallas_call(
        functools.partial(_qkv_kernel, n_norm_tiles=n_norm // tn, tn=tn),
        out_shape=jax.ShapeDtypeStruct((t, n), F32),
        grid=(t // tm, n // tn),
        in_specs=[
            pl.BlockSpec((tm, d), lambda i, j: (i, 0)),
            _mod_spec(layer, 0, d, cond_fn),
            _mod_spec(layer, 1, d, cond_fn),
            pl.BlockSpec((1, d), lambda i, j: (0, 0)),
            pl.BlockSpec((d, tn), lambda i, j: (0, j)),
            pl.BlockSpec((1, tn), lambda i, j: (0, j)),
            pl.BlockSpec((tm, LANE), rope_idx),
            pl.BlockSpec((tm, LANE), rope_idx),
            pl.BlockSpec((LANE, LANE), lambda i, j: (0, 0)),
        ],
        out_specs=pl.BlockSpec((tm, tn), lambda i, j: (i, j)),
        scratch_shapes=[pltpu.VMEM((tm, d), BF16)],
        compiler_params=_params(2),
        name=f"qkv_project_{layer}",
    )(x, mod, mod, g_mix.reshape(1, d), w, gain, cos_tab, sin_tab, grp)


def _rope_tables(n_lat, tm):
    rows = n_lat // GRID_W
    row = jnp.broadcast_to(jnp.arange(rows, dtype=F32)[:, None], (rows, GRID_W)).reshape(-1)
    col = jnp.broadcast_to(jnp.arange(GRID_W, dtype=F32)[None, :], (rows, GRID_W)).reshape(-1)
    quarter = HEAD // 4
    inv = ROPE_BASE ** (-jnp.arange(quarter, dtype=F32) / quarter)
    ang_r = row[:, None] * inv[None, :]
    ang_c = col[:, None] * inv[None, :]
    ang = jnp.concatenate([ang_r, ang_r, ang_c, ang_c], axis=-1)
    sign = jnp.where((jnp.arange(HEAD) % 32) < 16, -1.0, 1.0).astype(F32)
    cos = jnp.cos(ang)
    sin = jnp.sin(ang) * sign
    cos = jnp.concatenate([cos, cos], axis=-1)
    sin = jnp.concatenate([sin, sin], axis=-1)
    cos = jnp.concatenate([jnp.ones((tm, LANE), F32), cos], axis=0)
    sin = jnp.concatenate([jnp.zeros((tm, LANE), F32), sin], axis=0)
    return cos, sin


def _attn_a_kernel(lamp_ref, sub_ref, q_ref, *rest, n_ctx, n_own, tq, tk, heads, lambda_init):
    if n_ctx:
        ck_ref, cv_ref, k_ref, v_ref, o_ref, kb, vb, acc = rest
    else:
        k_ref, v_ref, o_ref, kb, vb, acc = rest
    qi = pl.program_id(2)
    n_keys = n_own + n_ctx

    @pl.when(qi == 0)
    def _():
        kb[0:n_own, :] = k_ref[...].astype(BF16)
        vb[0:n_own, :] = v_ref[...].astype(BF16)
        if n_ctx:
            kb[n_own:n_keys, :] = ck_ref[...].astype(BF16)
            vb[n_own:n_keys, :] = cv_ref[...].astype(BF16)

    lp = lamp_ref[...]
    lam = (jnp.exp(jnp.sum(lp[0:1] * lp[1:2], axis=-1, keepdims=True))
           - jnp.exp(jnp.sum(lp[2:3] * lp[3:4], axis=-1, keepdims=True)) + lambda_init)
    lo = lax.broadcasted_iota(I32, (1, LANE), 1) < HEAD
    n_main = n_own // tk
    tail = n_keys - n_main * tk

    cols = [slice(h * LANE, (h + 1) * LANE) for h in range(heads)]
    q2 = []
    for h in range(heads):
        q = q_ref[:, cols[h]] * LOG2E
        q2.append(jnp.concatenate([jnp.where(lo, q, 0.0), jnp.where(lo, 0.0, q)],
                                  axis=0).astype(BF16))
    acc[...] = jnp.zeros_like(acc)

    def step(rows, carry):
        out = []
        for h in range(heads):
            m, l = carry[h]
            s = _dot_nt(q2[h], kb[rows, cols[h]])
            m_new = jnp.maximum(m, jnp.max(s, axis=-1, keepdims=True))
            alpha = jnp.exp2(m - m_new)
            p = jnp.exp2(s - m_new)
            acc[h] = alpha * acc[h] + _dot(p.astype(BF16), vb[rows, cols[h]])
            out.append((m_new, alpha * l + jnp.sum(p, axis=-1, keepdims=True)))
        return tuple(out)

    def body(j, carry):
        return step(pl.ds(pl.multiple_of(j * tk, tk), tk), carry)

    carry = tuple((jnp.full((2 * tq, 1), NEG_INF, F32), jnp.zeros((2 * tq, 1), F32))
                  for _ in range(heads))
    if n_main:
        carry = lax.fori_loop(0, n_main, body, carry)
    if tail:
        carry = step(slice(n_main * tk, n_keys), carry)
    for h in range(heads):
        on = acc[h] / carry[h][1]
        o = on[0:tq] - lam * on[tq:2 * tq]
        o = o * lax.rsqrt(jnp.mean(o * o, axis=-1, keepdims=True) + NORM_EPS)
        o_ref[:, cols[h]] = (o * sub_ref[...] * (1.0 - lambda_init)).astype(BF16)


def _diff_attention(qkv, lamp, sub_gain, cache_k, cache_v, *, row0, n_batch, n_own, heads_total,
                    lambda_init):
    width = heads_total * LANE
    heads = 2 if heads_total % 2 == 0 else 1
    gw = heads * LANE
    n_groups = heads_total // heads
    tq = _pick(n_own, (256, 128))
    tk = _pick(n_own, (2048, 1024, 512, 256, 128))
    n_ctx = 0 if cache_k is None else cache_k.shape[0] // n_batch
    q_tiles = n_own // tq
    q_row0 = row0 // tq
    kv_row0 = row0 // n_own
    sec = width // gw

    in_specs = [
        pl.BlockSpec((4, HEAD), lambda b, g, i: (0, 0)),
        pl.BlockSpec((1, LANE), lambda b, g, i: (0, 0)),
        pl.BlockSpec((tq, gw), lambda b, g, i: (q_row0 + b * q_tiles + i, g)),
    ]
    args = [lamp, sub_gain.reshape(1, LANE), qkv]
    if n_ctx:
        in_specs += [pl.BlockSpec((n_ctx, gw), lambda b, g, i: (b, g)),
                     pl.BlockSpec((n_ctx, gw), lambda b, g, i: (b, g))]
        args += [cache_k, cache_v]
    in_specs += [pl.BlockSpec((n_own, gw), lambda b, g, i: (kv_row0 + b, sec + g)),
                 pl.BlockSpec((n_own, gw), lambda b, g, i: (kv_row0 + b, 2 * sec + g))]
    args += [qkv, qkv]
    n_keys = n_own + n_ctx
    return pl.pallas_call(
        functools.partial(_attn_a_kernel, n_ctx=n_ctx, n_own=n_own, tq=tq, tk=tk, heads=heads,
                          lambda_init=lambda_init),
        out_shape=jax.ShapeDtypeStruct((n_batch * n_own, width), BF16),
        grid=(n_batch, n_groups, q_tiles),
        in_specs=in_specs,
        out_specs=pl.BlockSpec((tq, gw), lambda b, g, i: (b * q_tiles + i, g)),
        scratch_shapes=[pltpu.VMEM((n_keys, gw), BF16), pltpu.VMEM((n_keys, gw), BF16),
                        pltpu.VMEM((heads, 2 * tq, LANE), F32)],
        compiler_params=_params(3),
        name="diff_attention_ctx" if n_ctx else "diff_attention_self",
    )(*args)


def _attn_b_kernel(sink_ref, q_ref, *rest, n_ctx, n_own, tq, windowed, slabs):
    if n_ctx:
        ck_ref, cv_ref, k_ref, v_ref, o_ref, kb, vb = rest
    else:
        k_ref, v_ref, o_ref, kb, vb = rest
    g = pl.program_id(1)
    qi = pl.program_id(2)
    pad = WINDOW if windowed else 0
    own0 = n_ctx + pad

    @pl.when(qi == 0)
    def _():
        if n_ctx:
            kb[0:n_ctx, :] = ck_ref[...].astype(BF16)
            vb[0:n_ctx, :] = cv_ref[...].astype(BF16)
        if pad:
            zeros = jnp.zeros((pad, LANE), BF16)
            for ref in (kb, vb):
                ref[n_ctx:own0, :] = zeros
                ref[own0 + n_own:own0 + n_own + pad, :] = zeros
        kb[own0:own0 + n_own, :] = k_ref[...].astype(BF16)
        vb[own0:own0 + n_own, :] = v_ref[...].astype(BF16)

    if windowed:
        wlen = tq + 2 * pad
        st = pl.multiple_of(n_ctx + qi * tq, tq)
        k_w = kb[pl.ds(st, wlen), :]
        v_w = vb[pl.ds(st, wlen), :]
        r = lax.broadcasted_iota(I32, (2 * tq, wlen), 0) & (tq - 1)
        c = lax.broadcasted_iota(I32, (2 * tq, wlen), 1)
        kpos = qi * tq - pad + c
        d = c - r
        mask = (d >= 0) & (d <= 2 * pad) & (kpos >= 0) & (kpos < n_own)
    else:
        k_w = kb[own0:own0 + n_own, :]
        v_w = vb[own0:own0 + n_own, :]
        mask = None
    if n_ctx:
        k_c = kb[0:n_ctx, :]
        v_c = vb[0:n_ctx, :]

    lo = lax.broadcasted_iota(I32, (1, LANE), 1) < HEAD
    top = lax.broadcasted_iota(I32, (2 * tq, 1), 0) < tq
    for t in range(slabs):
        cols = slice(t * LANE, (t + 1) * LANE)
        q = q_ref[:, cols] * LOG2E
        q2 = jnp.concatenate([jnp.where(lo, q, 0.0), jnp.where(lo, 0.0, q)], axis=0).astype(BF16)
        head0 = g * (2 * slabs) + 2 * t
        sk = jnp.where(top, sink_ref[head0], sink_ref[head0 + 1]) * LOG2E
        s_w = _dot_nt(q2, k_w)
        if mask is not None:
            s_w = jnp.where(mask, s_w, NEG_INF)
        m = jnp.maximum(jnp.max(s_w, axis=-1, keepdims=True), sk)
        if n_ctx:
            s_c = _dot_nt(q2, k_c)
            m = jnp.maximum(m, jnp.max(s_c, axis=-1, keepdims=True))
        p_w = jnp.exp2(s_w - m)
        den = jnp.sum(p_w, axis=-1, keepdims=True) + jnp.exp2(sk - m)
        o = _dot(p_w.astype(BF16), v_w)
        if n_ctx:
            p_c = jnp.exp2(s_c - m)
            den = den + jnp.sum(p_c, axis=-1, keepdims=True)
            o = o + _dot(p_c.astype(BF16), v_c)
        o = o / den
        o_ref[:, cols] = jnp.where(lo, o[0:tq], o[tq:2 * tq]).astype(BF16)


def _sink_attention(qkv, sinks, cache_k, cache_v, *, row0, n_batch, n_own, q_width, kv_heads,
                    windowed):
    gw = q_width // kv_heads
    slabs = gw // LANE
    tq = _pick(n_own, (256, 128))
    n_ctx = 0 if cache_k is None else cache_k.shape[0] // n_batch
    q_tiles = n_own // tq
    q_row0 = row0 // tq
    kv_row0 = row0 // n_own
    k_col0 = q_width // LANE
    v_col0 = k_col0 + kv_heads
    pad = WINDOW if windowed else 0

    in_specs = [
        pl.BlockSpec(memory_space=pltpu.SMEM),
        pl.BlockSpec((tq, gw), lambda b, g, i: (q_row0 + b * q_tiles + i, g)),
    ]
    args = [sinks, qkv]
    if n_ctx:
        in_specs += [pl.BlockSpec((n_ctx, LANE), lambda b, g, i: (b, g)),
                     pl.BlockSpec((n_ctx, LANE), lambda b, g, i: (b, g))]
        args += [cache_k, cache_v]
    in_specs += [pl.BlockSpec((n_own, LANE), lambda b, g, i: (kv_row0 + b, k_col0 + g)),
                 pl.BlockSpec((n_own, LANE), lambda b, g, i: (kv_row0 + b, v_col0 + g))]
    args += [qkv, qkv]
    n_rows = n_ctx + n_own + 2 * pad
    return pl.pallas_call(
        functools.partial(_attn_b_kernel, n_ctx=n_ctx, n_own=n_own, tq=tq, windowed=windowed,
                          slabs=slabs),
        out_shape=jax.ShapeDtypeStruct((n_batch * n_own, q_width), BF16),
        grid=(n_batch, kv_heads, q_tiles),
        in_specs=in_specs,
        out_specs=pl.BlockSpec((tq, gw), lambda b, g, i: (b * q_tiles + i, g)),
        scratch_shapes=[pltpu.VMEM((n_rows, LANE), BF16), pltpu.VMEM((n_rows, LANE), BF16)],
        compiler_params=_params(3),
        name="sink_attention_window" if windowed else "sink_attention_self",
    )(*args)


def _out_kernel(o_ref, wo_ref, x_ref, gt_ref, g_ref, sh_ref, sc_ref, wr_ref, br_ref,
                x1_ref, hp_ref, route_ref, gate_ref, cnt_ref, carry, *, n_exp, tm):
    i = pl.program_id(0)

    @pl.when(i == 0)
    def _():
        carry[...] = jnp.zeros_like(carry)

    x1 = x_ref[...] + gt_ref[...] * _dot(o_ref[...], wo_ref[...])
    x1_ref[...] = x1
    h = _modulate(x1, g_ref[...], sh_ref[...], sc_ref[...])
    h_hi, h_lo = _split(h)
    half = h.shape[1] // 2
    ua = pltpu.bitcast(h_hi[:, :half].astype(F32), U32)
    ub = pltpu.bitcast(h_hi[:, half:].astype(F32), U32)
    hp_ref[...] = ua | (ub >> 16)

    w_hi, w_lo = _split(wr_ref[...])
    logits = _dot(h_hi, w_hi) + _dot(h_lo, w_hi) + _dot(h_hi, w_lo) + br_ref[...]

    eio = lax.broadcasted_iota(I32, (tm, n_exp), 1).astype(F32)
    work = logits
    tops, idxs, hots = [], [], []
    for _ in range(TOP_K):
        mk = jnp.max(work, axis=-1, keepdims=True)
        ik = jnp.min(jnp.where(work == mk, eio, float(n_exp)), axis=-1, keepdims=True)
        hot = eio == ik
        tops.append(mk)
        idxs.append(ik)
        hots.append(hot)
        work = jnp.where(hot, -3e38, work)
    es = [jnp.exp(t - tops[0]) for t in tops]
    den = es[0] + es[1] + es[2] + es[3]

    hot_all = jnp.zeros((tm, n_exp), F32)
    for hot in hots:
        hot_all = hot_all + jnp.where(hot, 1.0, 0.0)
    tri = jnp.where(lax.broadcasted_iota(I32, (tm, tm), 0) > lax.broadcasted_iota(I32, (tm, tm), 1),
                    1.0, 0.0).astype(BF16)
    base = _dot(tri, hot_all.astype(BF16)) + carry[...]
    ranks = [jnp.sum(jnp.where(hot, base, 0.0), axis=-1, keepdims=True).astype(I32) for hot in hots]
    carry[...] = carry[...] + jnp.sum(hot_all, axis=0, keepdims=True)
    cnt_ref[...] = carry[...]

    col = lax.broadcasted_iota(I32, (tm, LANE), 1)
    route = jnp.zeros((tm, LANE), I32)
    gate = jnp.zeros((tm, LANE), F32)
    for k in range(TOP_K):
        route = jnp.where(col == k, idxs[k].astype(I32), route)
        route = jnp.where(col == TOP_K + k, ranks[k], route)
        gate = jnp.where(col == k, es[k] / den, gate)
    route_ref[...] = route
    gate_ref[...] = gate


def _out_project(o, w_o, x, mod, layer, g_ffn, w_router, b_router, *, t_ctx, n_lat):
    t, d = x.shape
    kdim = o.shape[1]
    n_exp = w_router.shape[1]
    tm = _pick(math.gcd(t_ctx, n_lat), (512, 256, 128))
    cond_fn = functools.partial(_cond_of_tile, n_ctx_tiles=t_ctx // tm, tiles_per_seq=n_lat // tm)
    row = lambda i: (i, 0)
    fixed = lambda i: (0, 0)
    return pl.pallas_call(
        functools.partial(_out_kernel, n_exp=n_exp, tm=tm),
        out_shape=(jax.ShapeDtypeStruct((t, d), F32), jax.ShapeDtypeStruct((t, d // 2), U32),
                   jax.ShapeDtypeStruct((t, LANE), I32), jax.ShapeDtypeStruct((t, LANE), F32),
                   jax.ShapeDtypeStruct((1, n_exp), F32)),
        grid=(t // tm,),
        in_specs=[
            pl.BlockSpec((tm, kdim), row),
            pl.BlockSpec((kdim, d), fixed),
            pl.BlockSpec((tm, d), row),
            _mod_spec(layer, 2, d, cond_fn),
            pl.BlockSpec((1, d), fixed),
            _mod_spec(layer, 3, d, cond_fn),
            _mod_spec(layer, 4, d, cond_fn),
            pl.BlockSpec((d, n_exp), fixed),
            pl.BlockSpec((1, n_exp), fixed),
        ],
        out_specs=(pl.BlockSpec((tm, d), row), pl.BlockSpec((tm, d // 2), row),
                   pl.BlockSpec((tm, LANE), row), pl.BlockSpec((tm, LANE), row),
                   pl.BlockSpec((1, n_exp), fixed)),
        scratch_shapes=[pltpu.VMEM((1, n_exp), F32)],
        compiler_params=_params(1),
        name=f"out_project_route_{layer}",
    )(o, w_o, x, mod, g_ffn.reshape(1, d), mod, mod, w_router, b_router.reshape(1, n_exp))


def _moe_kernel(te_ref, nv_ref, tok_hbm, hp_hbm, wg_ref, wu_ref, bg_ref, bu_ref, wo_ref, bo_ref,
                y_ref, idx, xs, xb, act, isem, gsem, *, tm, nf1, rows_per_step):
    i = pl.program_id(0)
    s = pl.program_id(1)
    nv = nv_ref[0]
    cur = i % 2
    nxt = 1 - cur

    def fetch_idx(tile, buf):
        cp = pltpu.make_async_copy(tok_hbm.at[tile], idx.at[buf], isem)
        cp.start()
        cp.wait()

    def row_copy(buf, r):
        return pltpu.make_async_copy(hp_hbm.at[pl.ds(idx[buf, 0, r], 1)], xs.at[buf, pl.ds(r, 1)],
                                     gsem.at[buf])

    @pl.when(jnp.logical_and(i == 0, s == 0))
    def _():
        fetch_idx(0, 0)

        def issue(r, c):
            row_copy(0, r).start()
            return c

        lax.fori_loop(0, tm, issue, 0)

    @pl.when(jnp.logical_and(i < nv, s == 0))
    def _():
        def drain(r, c):
            row_copy(cur, r).wait()
            return c

        lax.fori_loop(0, tm, drain, 0)
        half = xs.shape[2]
        u = xs[cur]
        xb[:, :half] = pltpu.bitcast(u & jnp.uint32(0xFFFF0000), F32).astype(BF16)
        xb[:, half:] = pltpu.bitcast(u << 16, F32).astype(BF16)

    @pl.when(jnp.logical_and(i + 1 < nv, s == 0))
    def _():
        fetch_idx(i + 1, nxt)

    def activation_chunk():
        wg = wg_ref[...].astype(BF16)
        wu = wu_ref[...].astype(BF16)
        for r in range(2):
            rows = slice(r * (tm // 2), (r + 1) * (tm // 2))
            x = xb[rows, :]
            g = _dot(x, wg) + bg_ref[...]
            u = _dot(x, wu) + bu_ref[...]
            g = jnp.minimum(g, SWIGLU_LIMIT)
            u = jnp.clip(u, -SWIGLU_LIMIT, SWIGLU_LIMIT)
            act[s, rows, :] = (g * jax.nn.sigmoid(SWIGLU_ALPHA * g) * (u + 1.0)).astype(BF16)

    @pl.when(jnp.logical_and(i + 1 < nv, s < nf1))
    def _():
        r0 = s * rows_per_step
        for r in range(rows_per_step):
            row_copy(nxt, r0 + r).start()
        activation_chunk()

    @pl.when(jnp.logical_and(i + 1 == nv, s < nf1))
    def _():
        activation_chunk()

    @pl.when(jnp.logical_and(i < nv, s >= nf1))
    def _():
        a = jnp.concatenate([act[c] for c in range(nf1)], axis=1)
        y_ref[...] = _dot(a, wo_ref[...].astype(BF16)) + bo_ref[...]

    @pl.when(jnp.logical_and(i >= nv, s >= nf1))
    def _():
        y_ref[...] = jnp.zeros_like(y_ref)


def _moe_experts(hp, slot_token, tile_expert, n_valid, layer, w_in, b_in, w_out, b_out, *, tm):
    cap = slot_token.shape[0]
    half = hp.shape[1]
    depth, n_exp, d, two_ff = w_in.shape
    d_ff = two_ff // 2
    tf = _pick(d_ff, (256, 128))
    nf1 = d_ff // tf
    tn = _pick(d, (512, 256, 128))
    nf2 = d // tn
    rows_per_step = tm // nf1

    def in_chunk(i, s, nv):
        return jnp.where(i < nv[0], jnp.minimum(s, nf1 - 1), nf1 - 1)

    def out_chunk(i, s, nv):
        return jnp.where(i < nv[0], jnp.maximum(s - nf1, 0), nf2 - 1)

    def wo_idx(i, s, te, nv):
        hold = jnp.logical_and(s == 0, i > 0)
        e = jnp.where(hold, te[jnp.maximum(i - 1, 0)], te[i])
        return (layer, e, 0, jnp.where(hold, nf2 - 1, out_chunk(i, s, nv)))

    wblk = (None, None, d, tf)
    bblk = (None, None, 1, tf)
    grid_spec = pltpu.PrefetchScalarGridSpec(
        num_scalar_prefetch=2,
        grid=(cap // tm, nf1 + nf2),
        in_specs=[
            pl.BlockSpec(memory_space=pl.ANY),
            pl.BlockSpec(memory_space=pl.ANY),
            pl.BlockSpec(wblk, lambda i, s, te, nv: (layer, te[i], 0, in_chunk(i, s, nv))),
            pl.BlockSpec(wblk, lambda i, s, te, nv: (layer, te[i], 0, nf1 + in_chunk(i, s, nv))),
            pl.BlockSpec(bblk, lambda i, s, te, nv: (layer, te[i], 0, in_chunk(i, s, nv))),
            pl.BlockSpec(bblk, lambda i, s, te, nv: (layer, te[i], 0, nf1 + in_chunk(i, s, nv))),
            pl.BlockSpec((None, None, d_ff, tn), wo_idx),
            pl.BlockSpec((None, None, 1, tn),
                         lambda i, s, te, nv: (layer, te[i], 0, out_chunk(i, s, nv))),
        ],
        out_specs=pl.BlockSpec((tm, tn), lambda i, s, te, nv: (i, jnp.maximum(s - nf1, 0))),
        scratch_shapes=[pltpu.SMEM((2, 1, tm), I32), pltpu.VMEM((2, tm, half), U32),
                        pltpu.VMEM((tm, d), BF16), pltpu.VMEM((nf1, tm, tf), BF16),
                        pltpu.SemaphoreType.DMA, pltpu.SemaphoreType.DMA((2,))],
    )
    b_in4 = b_in.reshape(depth, n_exp, 1, two_ff)
    return pl.pallas_call(
        functools.partial(_moe_kernel, tm=tm, nf1=nf1, rows_per_step=rows_per_step),
        out_shape=jax.ShapeDtypeStruct((cap, d), F32),
        grid_spec=grid_spec,
        compiler_params=_params(2),
        name=f"moe_experts_{layer}",
    )(tile_expert, n_valid, slot_token.reshape(cap // tm, 1, tm), hp, w_in, w_in, b_in4, b_in4,
      w_out, b_out.reshape(depth, n_exp, 1, d))


def _combine_kernel(slot_hbm, gate_ref, x_ref, gt_ref, y_hbm, o_ref, idx, buf, isem, sem, *, tc):
    i = pl.program_id(0)
    n = pl.num_programs(0)
    cur = i % 2
    nxt = 1 - cur

    def row_copy(b, r, k):
        return pltpu.make_async_copy(y_hbm.at[pl.ds(idx[b, 0, r * TOP_K + k], 1)],
                                     buf.at[b, k, pl.ds(r, 1)], sem.at[b])

    def start_tile(tile, b):
        cp = pltpu.make_async_copy(slot_hbm.at[tile], idx.at[b], isem)
        cp.start()
        cp.wait()

        def issue(r, c):
            for k in range(TOP_K):
                row_copy(b, r, k).start()
            return c

        lax.fori_loop(0, tc, issue, 0)

    @pl.when(i == 0)
    def _():
        start_tile(0, 0)

    @pl.when(i + 1 < n)
    def _():
        start_tile(i + 1, nxt)

    def drain(r, c):
        for k in range(TOP_K):
            row_copy(cur, r, k).wait()
        return c

    lax.fori_loop(0, tc, drain, 0)
    gate = gate_ref[...]
    f = gate[:, 0:1] * buf[cur, 0]
    for k in range(1, TOP_K):
        f = f + gate[:, k:k + 1] * buf[cur, k]
    o_ref[...] = x_ref[...] + gt_ref[...] * f


def _moe_combine(slot, gates, x1, mod, layer, y_sorted, *, t_ctx, n_lat):
    t, d = x1.shape
    tc = _pick(math.gcd(t_ctx, n_lat), (128,))
    cond_fn = functools.partial(_cond_of_tile, n_ctx_tiles=t_ctx // tc, tiles_per_seq=n_lat // tc)
    row = lambda i: (i, 0)
    return pl.pallas_call(
        functools.partial(_combine_kernel, tc=tc),
        out_shape=jax.ShapeDtypeStruct((t, d), F32),
        grid=(t // tc,),
        in_specs=[
            pl.BlockSpec(memory_space=pl.ANY),
            pl.BlockSpec((tc, LANE), row),
            pl.BlockSpec((tc, d), row),
            _mod_spec(layer, 5, d, cond_fn),
            pl.BlockSpec(memory_space=pl.ANY),
        ],
        out_specs=pl.BlockSpec((tc, d), row),
        scratch_shapes=[pltpu.SMEM((2, 1, tc * TOP_K), I32), pltpu.VMEM((2, TOP_K, tc, d), F32),
                        pltpu.SemaphoreType.DMA, pltpu.SemaphoreType.DMA((2,))],
        compiler_params=_params(1),
        name="moe_combine",
    )(slot.reshape(t // tc, 1, tc * TOP_K), gates, x1, mod, y_sorted)


def _moe_tile(n_assign, n_exp):
    per_expert = n_assign // n_exp
    for tm in (1024, 512, 256):
        if per_expert >= 3 * tm:
            return tm
    return 128


def _moe_ffn(hp, route, gates, counts, x1, mod, layer, w_in, b_in, w_out, b_out, *, t_ctx, n_lat):
    t = x1.shape[0]
    n_exp = w_in.shape[1]
    n_assign = t * TOP_K
    tm = _moe_tile(n_assign, n_exp)
    cap = -(-(n_assign + n_exp * (tm - 1)) // tm) * tm
    n_tiles = cap // tm

    counts = counts.reshape(n_exp).astype(I32)
    padded = (counts + tm - 1) // tm * tm
    pend = jnp.cumsum(padded)
    pstart = pend - padded
    idx = route[:, :TOP_K]
    slot = pstart[idx] + route[:, TOP_K:2 * TOP_K]
    token_of = jnp.broadcast_to(jnp.arange(t, dtype=I32)[:, None], (t, TOP_K))
    slot_token = jnp.zeros((cap,), I32).at[slot.reshape(-1)].set(
        token_of.reshape(-1), unique_indices=True)
    n_valid = (pend[-1] // tm).astype(I32)
    tile_start = jnp.minimum(jnp.arange(n_tiles, dtype=I32), n_valid - 1) * tm
    tile_expert = jnp.minimum(jnp.searchsorted(pend, tile_start, side='right'),
                              n_exp - 1).astype(I32)

    y_sorted = _moe_experts(hp, slot_token, tile_expert, n_valid.reshape(1), layer, w_in, b_in,
                            w_out, b_out, tm=tm)
    return _moe_combine(slot, gates, x1, mod, layer, y_sorted, t_ctx=t_ctx, n_lat=n_lat)


def _head_table(gain, n_heads, scale=1.0):
    return jnp.tile(gain * scale, n_heads)


def kernel(x_prompt, x_sample, cache_a_k, cache_a_v, cache_b_k, cache_b_v, c, c_ctx, w_ada, b_ada, g_mix, g_ffn, w_qkv_a, q_norm_a, k_norm_a, lam_q1_a, lam_k1_a, lam_q2_a, lam_k2_a, subln_a, w_o_a, w_qkv_b, q_norm_b, k_norm_b, sinks_b, w_o_b, w_router, b_router, w_in, b_in, w_out, b_out):
    batch, seq, d = x_prompt.shape
    dec_batch, n_lat, _ = x_sample.shape
    depth = w_ada.shape[0]
    past = cache_a_k.shape[2]
    heads_a = cache_a_k.shape[3]
    kv_heads_b = cache_b_k.shape[3]
    q_heads_b = sinks_b.shape[1]
    t_ctx = batch * seq
    t_lat = dec_batch * n_lat
    assert dec_batch + 1 <= N_COND

    x = jnp.concatenate([x_prompt.reshape(t_ctx, d), x_sample.reshape(t_lat, d)], axis=0)
    cond = jnp.concatenate([c_ctx[None, :], c, jnp.zeros((N_COND - 1 - dec_batch, d), F32)], axis=0)
    mod = _ada_params(cond, w_ada, b_ada).reshape(depth, N_COND, 6, 1, d)

    tm_qkv = _pick(math.gcd(t_ctx, n_lat), (1024, 512, 256, 128))
    cos_tab, sin_tab = _rope_tables(n_lat, tm_qkv)
    grp = jnp.where((jnp.arange(LANE)[:, None] // HEAD) == (jnp.arange(LANE)[None, :] // HEAD),
                    1.0 / HEAD, 0.0).astype(BF16)

    wa = heads_a * LANE
    qw_b = q_heads_b * HEAD
    kvw_b = kv_heads_b * HEAD
    states = {"a_k": [], "a_v": [], "b_k": [], "b_v": []}

    for layer in range(depth):
        i = layer // 2
        if layer % 2 == 0:
            lambda_init = 0.8 - 0.6 * math.exp(-0.3 * layer)
            gain = jnp.concatenate([
                _head_table(q_norm_a[i], 2 * heads_a, HEAD ** -0.5),
                _head_table(k_norm_a[i], 2 * heads_a),
                jnp.ones((wa,), F32)]).reshape(1, 3 * wa)
            qkv = _qkv_project(x, mod, layer, g_mix[layer], w_qkv_a[i].astype(BF16), gain,
                               cos_tab, sin_tab, grp, tm=tm_qkv, n_norm=2 * wa, t_ctx=t_ctx, n_lat=n_lat)
            states["a_k"].append(qkv[:t_ctx, wa:2 * wa].reshape(batch, seq, heads_a, 2, HEAD))
            states["a_v"].append(qkv[:t_ctx, 2 * wa:].reshape(batch, seq, heads_a, LANE))
            lamp = jnp.stack([lam_q1_a[i], lam_k1_a[i], lam_q2_a[i], lam_k2_a[i]])
            o_p = _diff_attention(qkv, lamp, subln_a[i], None, None, row0=0, n_batch=batch,
                                  n_own=seq, heads_total=heads_a, lambda_init=lambda_init)
            o_s = _diff_attention(qkv, lamp, subln_a[i],
                                  cache_a_k[:, i].reshape(dec_batch * past, wa),
                                  cache_a_v[:, i].reshape(dec_batch * past, wa),
                                  row0=t_ctx, n_batch=dec_batch, n_own=n_lat, heads_total=heads_a,
                                  lambda_init=lambda_init)
            w_o = w_o_a[i]
        else:
            wq, wk, wv = jnp.split(w_qkv_b[i], [qw_b, qw_b + kvw_b], axis=1)
            dup = lambda w: jnp.concatenate([w.reshape(d, kv_heads_b, 1, HEAD)] * 2, axis=2).reshape(
                d, 2 * kvw_b)
            w_b = jnp.concatenate([wq, dup(wk), dup(wv)], axis=1).astype(BF16)
            gain = jnp.concatenate([
                _head_table(q_norm_b[i], q_heads_b, HEAD ** -0.5),
                _head_table(k_norm_b[i], 2 * kv_heads_b),
                jnp.ones((2 * kvw_b,), F32)]).reshape(1, qw_b + 4 * kvw_b)
            qkv = _qkv_project(x, mod, layer, g_mix[layer], w_b, gain, cos_tab, sin_tab, grp,
                               tm=tm_qkv, n_norm=qw_b + 2 * kvw_b, t_ctx=t_ctx, n_lat=n_lat)
            k_state = qkv[:t_ctx, qw_b:qw_b + 2 * kvw_b].reshape(batch, seq, kv_heads_b, 2, HEAD)
            v_state = qkv[:t_ctx, qw_b + 2 * kvw_b:].reshape(batch, seq, kv_heads_b, 2, HEAD)
            states["b_k"].append(k_state[:, :, :, 0])
            states["b_v"].append(v_state[:, :, :, 0])
            dup_c = lambda a: jnp.concatenate([a, a], axis=-1).reshape(dec_batch * past,
                                                                       2 * kvw_b)
            o_p = _sink_attention(qkv, sinks_b[i], None, None, row0=0, n_batch=batch, n_own=seq,
                                  q_width=qw_b, kv_heads=kv_heads_b, windowed=False)
            o_s = _sink_attention(qkv, sinks_b[i], dup_c(cache_b_k[:, i]), dup_c(cache_b_v[:, i]),
                                  row0=t_ctx, n_batch=dec_batch, n_own=n_lat, q_width=qw_b,
                                  kv_heads=kv_heads_b, windowed=True)
            w_o = w_o_b[i]

        o = jnp.concatenate([o_p, o_s], axis=0)
        x1, hp, route, gates, counts = _out_project(
            o, w_o.astype(BF16), x, mod, layer, g_ffn[layer], w_router[layer], b_router[layer],
            t_ctx=t_ctx, n_lat=n_lat)
        x = _moe_ffn(hp, route, gates, counts, x1, mod, layer, w_in, b_in, w_out, b_out,
                     t_ctx=t_ctx, n_lat=n_lat)

    return (x[:t_ctx].reshape(batch, seq, d), x[t_ctx:].reshape(dec_batch, n_lat, d),
            jnp.stack(states["a_k"], axis=1), jnp.stack(states["a_v"], axis=1),
            jnp.stack(states["b_k"], axis=1), jnp.stack(states["b_v"], axis=1))
```

```python
import functools
import math

import jax
import jax.numpy as jnp
from jax import lax
from jax.experimental import pallas as pl
from jax.experimental.pallas import tpu as pltpu

F32 = jnp.float32
BF16 = jnp.bfloat16
I32 = jnp.int32
U32 = jnp.uint32

LANE = 128
HEAD = 64
NORM_EPS = 1e-6
ROPE_BASE = 10000.0
GRID_W = 64
WINDOW = 128
TOP_K = 4
SWIGLU_LIMIT = 7.0
SWIGLU_ALPHA = 1.702
NEG_INF = -1e30
LOG2E = 1.4426950408889634
N_COND = 8
VMEM_LIMIT = 56 * 1024 * 1024


def _params(n_axes):
    return pltpu.CompilerParams(
        dimension_semantics=("arbitrary",) * n_axes, vmem_limit_bytes=VMEM_LIMIT)


def _pick(n, prefs):
    for p in prefs:
        if n % p == 0:
            return p
    raise ValueError(f"no tile in {prefs} divides {n}")


def _dot(a, b):
    return jnp.dot(a, b, preferred_element_type=F32)


def _dot_nt(a, b):
    return lax.dot_general(a, b, (((1,), (1,)), ((), ())), preferred_element_type=F32)


def _split(v):
    hi = v.astype(BF16)
    lo = (v - hi.astype(F32)).astype(BF16)
    return hi, lo


def _dot3(a, b):
    a_hi, a_lo = _split(a)
    b_hi, b_lo = _split(b)
    return _dot(a_hi, b_hi) + _dot(a_lo, b_hi) + _dot(a_hi, b_lo)


def _ada_kernel(c_ref, w_ref, b_ref, o_ref):
    c = c_ref[...]
    s = c * jax.nn.sigmoid(c)
    o_ref[...] = _dot3(s, w_ref[...]) + b_ref[...]


def _ada_params(cond, w_ada, b_ada):
    depth, d, n = w_ada.shape
    tn = _pick(n, (512, 256, 128))
    return pl.pallas_call(
        _ada_kernel,
        out_shape=jax.ShapeDtypeStruct((depth, N_COND, n), F32),
        grid=(depth, n // tn),
        in_specs=[
            pl.BlockSpec((N_COND, d), lambda l, j: (0, 0)),
            pl.BlockSpec((None, d, tn), lambda l, j: (l, 0, j)),
            pl.BlockSpec((None, 1, tn), lambda l, j: (l, 0, j)),
        ],
        out_specs=pl.BlockSpec((None, N_COND, tn), lambda l, j: (l, 0, j)),
        compiler_params=_params(2),
        name="ada_params",
    )(cond, w_ada, b_ada.reshape(depth, 1, n))


def _modulate(x, g, sh, sc):
    xn = x * lax.rsqrt(jnp.mean(x * x, axis=-1, keepdims=True) + NORM_EPS)
    return xn * g * (1.0 + sc) + sh


def _qkv_kernel(x_ref, sh_ref, sc_ref, g_ref, w_ref, gain_ref, cos_ref, sin_ref, grp_ref,
                o_ref, h_scr, *, n_norm_tiles, tn):
    j = pl.program_id(1)

    @pl.when(j == 0)
    def _():
        h_scr[...] = _modulate(x_ref[...], g_ref[...], sh_ref[...], sc_ref[...]).astype(BF16)

    y = _dot(h_scr[...], w_ref[...])

    @pl.when(j < n_norm_tiles)
    def _():
        lane = lax.broadcasted_iota(I32, (1, LANE), 1)
        first = (lane & 16) == 0
        cos = cos_ref[...]
        sin = sin_ref[...]
        for s in range(tn // LANE):
            cols = slice(s * LANE, (s + 1) * LANE)
            ys = y[:, cols]
            ms = _dot((ys * ys).astype(BF16), grp_ref[...])
            yn = ys * lax.rsqrt(ms + NORM_EPS) * gain_ref[:, cols]
            rot = jnp.where(first, pltpu.roll(yn, LANE - 16, 1), pltpu.roll(yn, 16, 1))
            o_ref[:, cols] = yn * cos + rot * sin

    @pl.when(j >= n_norm_tiles)
    def _():
        o_ref[...] = y


def _cond_of_tile(i, n_ctx_tiles, tiles_per_seq):
    return jnp.where(i < n_ctx_tiles, 0, 1 + (i - n_ctx_tiles) // tiles_per_seq)


def _mod_spec(layer, which, d, cond_fn):
    return pl.BlockSpec((None, None, None, 1, d),
                        lambda i, *_: (layer, cond_fn(i), which, 0, 0))


def _qkv_project(x, mod, layer, g_mix, w, gain, cos_tab, sin_tab, grp, *, tm, n_norm, t_ctx, n_lat):
    t, d = x.shape
    n = w.shape[1]
    tn = _pick(math.gcd(n_norm, n - n_norm), (512, 256, 128))
    n_ctx_tiles = t_ctx // tm
    tiles_per_seq = n_lat // tm
    cond_fn = functools.partial(_cond_of_tile, n_ctx_tiles=n_ctx_tiles, tiles_per_seq=tiles_per_seq)

    def rope_idx(i, j):
        return (jnp.where(i < n_ctx_tiles, 0, 1 + (i - n_ctx_tiles) % tiles_per_seq), 0)

    return pl.pallas_call(
        functools.partial(_qkv_kernel, n_norm_tiles=n_norm // tn, tn=tn),
        out_shape=jax.ShapeDtypeStruct((t, n), F32),
        grid=(t // tm, n // tn),
        in_specs=[
            pl.BlockSpec((tm, d), lambda i, j: (i, 0)),
            _mod_spec(layer, 0, d, cond_fn),
            _mod_spec(layer, 1, d, cond_fn),
            pl.BlockSpec((1, d), lambda i, j: (0, 0)),
            pl.BlockSpec((d, tn), lambda i, j: (0, j)),
            pl.BlockSpec((1, tn), lambda i, j: (0, j)),
            pl.BlockSpec((tm, LANE), rope_idx),
            pl.BlockSpec((tm, LANE), rope_idx),
            pl.BlockSpec((LANE, LANE), lambda i, j: (0, 0)),
        ],
        out_specs=pl.BlockSpec((tm, tn), lambda i, j: (i, j)),
        scratch_shapes=[pltpu.VMEM((tm, d), BF16)],
        compiler_params=_params(2),
        name=f"qkv_project_{layer}",
    )(x, mod, mod, g_mix.reshape(1, d), w, gain, cos_tab, sin_tab, grp)


def _rope_tables(n_lat, tm):
    rows = n_lat // GRID_W
    row = jnp.broadcast_to(jnp.arange(rows, dtype=F32)[:, None], (rows, GRID_W)).reshape(-1)
    col = jnp.broadcast_to(jnp.arange(GRID_W, dtype=F32)[None, :], (rows, GRID_W)).reshape(-1)
    quarter = HEAD // 4
    inv = ROPE_BASE ** (-jnp.arange(quarter, dtype=F32) / quarter)
    ang_r = row[:, None] * inv[None, :]
    ang_c = col[:, None] * inv[None, :]
    ang = jnp.concatenate([ang_r, ang_r, ang_c, ang_c], axis=-1)
    sign = jnp.where((jnp.arange(HEAD) % 32) < 16, -1.0, 1.0).astype(F32)
    cos = jnp.cos(ang)
    sin = jnp.sin(ang) * sign
    cos = jnp.concatenate([cos, cos], axis=-1)
    sin = jnp.concatenate([sin, sin], axis=-1)
    cos = jnp.concatenate([jnp.ones((tm, LANE), F32), cos], axis=0)
    sin = jnp.concatenate([jnp.zeros((tm, LANE), F32), sin], axis=0)
    return cos, sin


def _attn_a_kernel(lamp_ref, sub_ref, q_ref, *rest, n_ctx, n_own, tq, tk, heads, lambda_init):
    if n_ctx:
        ck_ref, cv_ref, k_ref, v_ref, o_ref, kb, vb, acc = rest
    else:
        k_ref, v_ref, o_ref, kb, vb, acc = rest
    qi = pl.program_id(2)
    n_keys = n_own + n_ctx

    @pl.when(qi == 0)
    def _():
        kb[0:n_own, :] = k_ref[...].astype(BF16)
        vb[0:n_own, :] = v_ref[...].astype(BF16)
        if n_ctx:
            kb[n_own:n_keys, :] = ck_ref[...].astype(BF16)
            vb[n_own:n_keys, :] = cv_ref[...].astype(BF16)

    lp = lamp_ref[...]
    lam = (jnp.exp(jnp.sum(lp[0:1] * lp[1:2], axis=-1, keepdims=True))
           - jnp.exp(jnp.sum(lp[2:3] * lp[3:4], axis=-1, keepdims=True)) + lambda_init)
    lo = lax.broadcasted_iota(I32, (1, LANE), 1) < HEAD
    n_main = n_own // tk
    tail = n_keys - n_main * tk

    cols = [slice(h * LANE, (h + 1) * LANE) for h in range(heads)]
    q2 = []
    for h in range(heads):
        q = q_ref[:, cols[h]] * LOG2E
        q2.append(jnp.concatenate([jnp.where(lo, q, 0.0), jnp.where(lo, 0.0, q)],
                                  axis=0).astype(BF16))
    acc[...] = jnp.zeros_like(acc)

    def step(rows, carry):
        out = []
        for h in range(heads):
            m, l = carry[h]
            s = _dot_nt(q2[h], kb[rows, cols[h]])
            m_new = jnp.maximum(m, jnp.max(s, axis=-1, keepdims=True))
            alpha = jnp.exp2(m - m_new)
            p = jnp.exp2(s - m_new)
            acc[h] = alpha * acc[h] + _dot(p.astype(BF16), vb[rows, cols[h]])
            out.append((m_new, alpha * l + jnp.sum(p, axis=-1, keepdims=True)))
        return tuple(out)

    def body(j, carry):
        return step(pl.ds(pl.multiple_of(j * tk, tk), tk), carry)

    carry = tuple((jnp.full((2 * tq, 1), NEG_INF, F32), jnp.zeros((2 * tq, 1), F32))
                  for _ in range(heads))
    if n_main:
        carry = lax.fori_loop(0, n_main, body, carry)
    if tail:
        carry = step(slice(n_main * tk, n_keys), carry)
    for h in range(heads):
        on = acc[h] / carry[h][1]
        o = on[0:tq] - lam * on[tq:2 * tq]
        o = o * lax.rsqrt(jnp.mean(o * o, axis=-1, keepdims=True) + NORM_EPS)
        o_ref[:, cols[h]] = (o * sub_ref[...] * (1.0 - lambda_init)).astype(BF16)


def _diff_attention(qkv, lamp, sub_gain, cache_k, cache_v, *, row0, n_batch, n_own, heads_total,
                    lambda_init):
    width = heads_total * LANE
    heads = 2 if heads_total % 2 == 0 else 1
    gw = heads * LANE
    n_groups = heads_total // heads
    tq = _pick(n_own, (256, 128))
    tk = _pick(n_own, (2048, 1024, 512, 256, 128))
    n_ctx = 0 if cache_k is None else cache_k.shape[0] // n_batch
    q_tiles = n_own // tq
    q_row0 = row0 // tq
    kv_row0 = row0 // n_own
    sec = width // gw

    in_specs = [
        pl.BlockSpec((4, HEAD), lambda b, g, i: (0, 0)),
        pl.BlockSpec((1, LANE), lambda b, g, i: (0, 0)),
        pl.BlockSpec((tq, gw), lambda b, g, i: (q_row0 + b * q_tiles + i, g)),
    ]
    args = [lamp, sub_gain.reshape(1, LANE), qkv]
    if n_ctx:
        in_specs += [pl.BlockSpec((n_ctx, gw), lambda b, g, i: (b, g)),
                     pl.BlockSpec((n_ctx, gw), lambda b, g, i: (b, g))]
        args += [cache_k, cache_v]
    in_specs += [pl.BlockSpec((n_own, gw), lambda b, g, i: (kv_row0 + b, sec + g)),
                 pl.BlockSpec((n_own, gw), lambda b, g, i: (kv_row0 + b, 2 * sec + g))]
    args += [qkv, qkv]
    n_keys = n_own + n_ctx
    return pl.pallas_call(
        functools.partial(_attn_a_kernel, n_ctx=n_ctx, n_own=n_own, tq=tq, tk=tk, heads=heads,
                          lambda_init=lambda_init),
        out_shape=jax.ShapeDtypeStruct((n_batch * n_own, width), BF16),
        grid=(n_batch, n_groups, q_tiles),
        in_specs=in_specs,
        out_specs=pl.BlockSpec((tq, gw), lambda b, g, i: (b * q_tiles + i, g)),
        scratch_shapes=[pltpu.VMEM((n_keys, gw), BF16), pltpu.VMEM((n_keys, gw), BF16),
                        pltpu.VMEM((heads, 2 * tq, LANE), F32)],
        compiler_params=_params(3),
        name="diff_attention_ctx" if n_ctx else "diff_attention_self",
    )(*args)


def _attn_b_kernel(sink_ref, q_ref, *rest, n_ctx, n_own, tq, windowed, slabs):
    if n_ctx:
        ck_ref, cv_ref, k_ref, v_ref, o_ref, kb, vb = rest
    else:
        k_ref, v_ref, o_ref, kb, vb = rest
    g = pl.program_id(1)
    qi = pl.program_id(2)
    pad = WINDOW if windowed else 0
    own0 = n_ctx + pad

    @pl.when(qi == 0)
    def _():
        if n_ctx:
            kb[0:n_ctx, :] = ck_ref[...].astype(BF16)
            vb[0:n_ctx, :] = cv_ref[...].astype(BF16)
        if pad:
            zeros = jnp.zeros((pad, LANE), BF16)
            for ref in (kb, vb):
                ref[n_ctx:own0, :] = zeros
                ref[own0 + n_own:own0 + n_own + pad, :] = zeros
        kb[own0:own0 + n_own, :] = k_ref[...].astype(BF16)
        vb[own0:own0 + n_own, :] = v_ref[...].astype(BF16)

    if windowed:
        wlen = tq + 2 * pad
        st = pl.multiple_of(n_ctx + qi * tq, tq)
        k_w = kb[pl.ds(st, wlen), :]
        v_w = vb[pl.ds(st, wlen), :]
        r = lax.broadcasted_iota(I32, (2 * tq, wlen), 0) & (tq - 1)
        c = lax.broadcasted_iota(I32, (2 * tq, wlen), 1)
        kpos = qi * tq - pad + c
        d = c - r
        mask = (d >= 0) & (d <= 2 * pad) & (kpos >= 0) & (kpos < n_own)
    else:
        k_w = kb[own0:own0 + n_own, :]
        v_w = vb[own0:own0 + n_own, :]
        mask = None
    if n_ctx:
        k_c = kb[0:n_ctx, :]
        v_c = vb[0:n_ctx, :]

    lo = lax.broadcasted_iota(I32, (1, LANE), 1) < HEAD
    top = lax.broadcasted_iota(I32, (2 * tq, 1), 0) < tq
    for t in range(slabs):
        cols = slice(t * LANE, (t + 1) * LANE)
        q = q_ref[:, cols] * LOG2E
        q2 = jnp.concatenate([jnp.where(lo, q, 0.0), jnp.where(lo, 0.0, q)], axis=0).astype(BF16)
        head0 = g * (2 * slabs) + 2 * t
        sk = jnp.where(top, sink_ref[head0], sink_ref[head0 + 1]) * LOG2E
        s_w = _dot_nt(q2, k_w)
        if mask is not None:
            s_w = jnp.where(mask, s_w, NEG_INF)
        m = jnp.maximum(jnp.max(s_w, axis=-1, keepdims=True), sk)
        if n_ctx:
            s_c = _dot_nt(q2, k_c)
            m = jnp.maximum(m, jnp.max(s_c, axis=-1, keepdims=True))
        p_w = jnp.exp2(s_w - m)
        den = jnp.sum(p_w, axis=-1, keepdims=True) + jnp.exp2(sk - m)
        o = _dot(p_w.astype(BF16), v_w)
        if n_ctx:
            p_c = jnp.exp2(s_c - m)
            den = den + jnp.sum(p_c, axis=-1, keepdims=True)
            o = o + _dot(p_c.astype(BF16), v_c)
        o = o / den
        o_ref[:, cols] = jnp.where(lo, o[0:tq], o[tq:2 * tq]).astype(BF16)


def _sink_attention(qkv, sinks, cache_k, cache_v, *, row0, n_batch, n_own, q_width, kv_heads,
                    windowed):
    gw = q_width // kv_heads
    slabs = gw // LANE
    tq = _pick(n_own, (256, 128))
    n_ctx = 0 if cache_k is None else cache_k.shape[0] // n_batch
    q_tiles = n_own // tq
    q_row0 = row0 // tq
    kv_row0 = row0 // n_own
    k_col0 = q_width // LANE
    v_col0 = k_col0 + kv_heads
    pad = WINDOW if windowed else 0

    in_specs = [
        pl.BlockSpec(memory_space=pltpu.SMEM),
        pl.BlockSpec((tq, gw), lambda b, g, i: (q_row0 + b * q_tiles + i, g)),
    ]
    args = [sinks, qkv]
    if n_ctx:
        in_specs += [pl.BlockSpec((n_ctx, LANE), lambda b, g, i: (b, g)),
                     pl.BlockSpec((n_ctx, LANE), lambda b, g, i: (b, g))]
        args += [cache_k, cache_v]
    in_specs += [pl.BlockSpec((n_own, LANE), lambda b, g, i: (kv_row0 + b, k_col0 + g)),
                 pl.BlockSpec((n_own, LANE), lambda b, g, i: (kv_row0 + b, v_col0 + g))]
    args += [qkv, qkv]
    n_rows = n_ctx + n_own + 2 * pad
    return pl.pallas_call(
        functools.partial(_attn_b_kernel, n_ctx=n_ctx, n_own=n_own, tq=tq, windowed=windowed,
                          slabs=slabs),
        out_shape=jax.ShapeDtypeStruct((n_batch * n_own, q_width), BF16),
        grid=(n_batch, kv_heads, q_tiles),
        in_specs=in_specs,
        out_specs=pl.BlockSpec((tq, gw), lambda b, g, i: (b * q_tiles + i, g)),
        scratch_shapes=[pltpu.VMEM((n_rows, LANE), BF16), pltpu.VMEM((n_rows, LANE), BF16)],
        compiler_params=_params(3),
        name="sink_attention_window" if windowed else "sink_attention_self",
    )(*args)


def _out_kernel(oc_ref, ol_ref, wo_ref, x_ref, gt_ref, g_ref, sh_ref, sc_ref, wr_ref, br_ref,
                x1_ref, hp_ref, route_ref, gate_ref, cnt_ref, carry, *, n_exp, tm, n_ctx_tiles):
    i = pl.program_id(0)

    @pl.when(i == 0)
    def _():
        carry[...] = jnp.zeros_like(carry)

    o = jnp.where(i < n_ctx_tiles, oc_ref[...], ol_ref[...])
    x1 = x_ref[...] + gt_ref[...] * _dot(o, wo_ref[...])
    x1_ref[...] = x1
    h = _modulate(x1, g_ref[...], sh_ref[...], sc_ref[...])
    h_hi, h_lo = _split(h)
    half = h.shape[1] // 2
    ua = pltpu.bitcast(h_hi[:, :half].astype(F32), U32)
    ub = pltpu.bitcast(h_hi[:, half:].astype(F32), U32)
    hp_ref[...] = ua | (ub >> 16)

    w_hi, w_lo = _split(wr_ref[...])
    logits = _dot(h_hi, w_hi) + _dot(h_lo, w_hi) + _dot(h_hi, w_lo) + br_ref[...]

    eio = lax.broadcasted_iota(I32, (tm, n_exp), 1).astype(F32)
    work = logits
    tops, idxs, hots = [], [], []
    for _ in range(TOP_K):
        mk = jnp.max(work, axis=-1, keepdims=True)
        ik = jnp.min(jnp.where(work == mk, eio, float(n_exp)), axis=-1, keepdims=True)
        hot = eio == ik
        tops.append(mk)
        idxs.append(ik)
        hots.append(hot)
        work = jnp.where(hot, -3e38, work)
    es = [jnp.exp(t - tops[0]) for t in tops]
    den = es[0] + es[1] + es[2] + es[3]

    hot_all = jnp.zeros((tm, n_exp), F32)
    for hot in hots:
        hot_all = hot_all + jnp.where(hot, 1.0, 0.0)
    tri = jnp.where(lax.broadcasted_iota(I32, (tm, tm), 0) > lax.broadcasted_iota(I32, (tm, tm), 1),
                    1.0, 0.0).astype(BF16)
    base = _dot(tri, hot_all.astype(BF16)) + carry[...]
    ranks = [jnp.sum(jnp.where(hot, base, 0.0), axis=-1, keepdims=True).astype(I32) for hot in hots]
    carry[...] = carry[...] + jnp.sum(hot_all, axis=0, keepdims=True)
    cnt_ref[...] = carry[...]

    col = lax.broadcasted_iota(I32, (tm, LANE), 1)
    route = jnp.zeros((tm, LANE), I32)
    gate = jnp.zeros((tm, LANE), F32)
    for k in range(TOP_K):
        route = jnp.where(col == k, idxs[k].astype(I32), route)
        route = jnp.where(col == TOP_K + k, ranks[k], route)
        gate = jnp.where(col == k, es[k] / den, gate)
    route_ref[...] = route
    gate_ref[...] = gate


def _out_project(o_ctx, o_lat, w_o, x, mod, layer, g_ffn, w_router, b_router, *, t_ctx, n_lat):
    t, d = x.shape
    kdim = o_ctx.shape[1]
    n_exp = w_router.shape[1]
    tm = _pick(math.gcd(t_ctx, n_lat), (512, 256, 128))
    n_ctx_tiles = t_ctx // tm
    cond_fn = functools.partial(_cond_of_tile, n_ctx_tiles=n_ctx_tiles, tiles_per_seq=n_lat // tm)
    row = lambda i: (i, 0)
    fixed = lambda i: (0, 0)
    return pl.pallas_call(
        functools.partial(_out_kernel, n_exp=n_exp, tm=tm, n_ctx_tiles=n_ctx_tiles),
        out_shape=(jax.ShapeDtypeStruct((t, d), F32), jax.ShapeDtypeStruct((t, d // 2), U32),
                   jax.ShapeDtypeStruct((t, LANE), I32), jax.ShapeDtypeStruct((t, LANE), F32),
                   jax.ShapeDtypeStruct((1, n_exp), F32)),
        grid=(t // tm,),
        in_specs=[
            pl.BlockSpec((tm, kdim), lambda i: (jnp.minimum(i, n_ctx_tiles - 1), 0)),
            pl.BlockSpec((tm, kdim), lambda i: (jnp.maximum(i - n_ctx_tiles, 0), 0)),
            pl.BlockSpec((kdim, d), fixed),
            pl.BlockSpec((tm, d), row),
            _mod_spec(layer, 2, d, cond_fn),
            pl.BlockSpec((1, d), fixed),
            _mod_spec(layer, 3, d, cond_fn),
            _mod_spec(layer, 4, d, cond_fn),
            pl.BlockSpec((d, n_exp), fixed),
            pl.BlockSpec((1, n_exp), fixed),
        ],
        out_specs=(pl.BlockSpec((tm, d), row), pl.BlockSpec((tm, d // 2), row),
                   pl.BlockSpec((tm, LANE), row), pl.BlockSpec((tm, LANE), row),
                   pl.BlockSpec((1, n_exp), fixed)),
        scratch_shapes=[pltpu.VMEM((1, n_exp), F32)],
        compiler_params=_params(1),
        name=f"out_project_route_{layer}",
    )(o_ctx, o_lat, w_o, x, mod, g_ffn.reshape(1, d), mod, mod, w_router,
      b_router.reshape(1, n_exp))


def _moe_kernel(te_ref, nv_ref, tok_hbm, hp_hbm, wg_ref, wu_ref, bg_ref, bu_ref, wo_ref, bo_ref,
                y_ref, idx, xs, xb, act, isem, gsem, *, tm, nf1, rows_per_step):
    i = pl.program_id(0)
    s = pl.program_id(1)
    nv = nv_ref[0]
    cur = i % 2
    nxt = 1 - cur

    def fetch_idx(tile, buf):
        cp = pltpu.make_async_copy(tok_hbm.at[tile], idx.at[buf], isem)
        cp.start()
        cp.wait()

    row_words = xs.shape[2]

    def row_copy(buf, r):
        start = pl.multiple_of(idx[buf, 0, r], row_words)
        return pltpu.make_async_copy(hp_hbm.at[pl.ds(start, row_words)], xs.at[buf, r], gsem.at[buf])

    @pl.when(jnp.logical_and(i == 0, s == 0))
    def _():
        fetch_idx(0, 0)

        def issue(r, c):
            row_copy(0, r).start()
            return c

        lax.fori_loop(0, tm, issue, 0)

    @pl.when(jnp.logical_and(i < nv, s == 0))
    def _():
        def drain(r, c):
            row_copy(cur, r).wait()
            return c

        lax.fori_loop(0, tm, drain, 0)
        u = xs[cur]
        xb[:, :row_words] = pltpu.bitcast(u & jnp.uint32(0xFFFF0000), F32).astype(BF16)
        xb[:, row_words:] = pltpu.bitcast(u << 16, F32).astype(BF16)

    @pl.when(jnp.logical_and(i + 1 < nv, s == 0))
    def _():
        fetch_idx(i + 1, nxt)

    def activation_chunk():
        wg = wg_ref[...].astype(BF16)
        wu = wu_ref[...].astype(BF16)
        for r in range(2):
            rows = slice(r * (tm // 2), (r + 1) * (tm // 2))
            x = xb[rows, :]
            g = _dot(x, wg) + bg_ref[...]
            u = _dot(x, wu) + bu_ref[...]
            g = jnp.minimum(g, SWIGLU_LIMIT)
            u = jnp.clip(u, -SWIGLU_LIMIT, SWIGLU_LIMIT)
            act[s, rows, :] = (g * jax.nn.sigmoid(SWIGLU_ALPHA * g) * (u + 1.0)).astype(BF16)

    @pl.when(jnp.logical_and(i + 1 < nv, s < nf1))
    def _():
        r0 = s * rows_per_step
        for r in range(rows_per_step):
            row_copy(nxt, r0 + r).start()
        activation_chunk()

    @pl.when(jnp.logical_and(i + 1 == nv, s < nf1))
    def _():
        activation_chunk()

    @pl.when(jnp.logical_and(i < nv, s >= nf1))
    def _():
        a = jnp.concatenate([act[c] for c in range(nf1)], axis=1)
        y_ref[...] = _dot(a, wo_ref[...].astype(BF16)) + bo_ref[...]

    @pl.when(jnp.logical_and(i >= nv, s >= nf1))
    def _():
        y_ref[...] = jnp.zeros_like(y_ref)


def _moe_experts(hp, slot_token, tile_expert, n_valid, layer, w_in, b_in, w_out, b_out, *, tm):
    cap = slot_token.shape[0]
    depth, n_exp, d, two_ff = w_in.shape
    d_ff = two_ff // 2
    tf = _pick(d_ff, (256, 128))
    nf1 = d_ff // tf
    tn = _pick(d, (512, 256, 128))
    nf2 = d // tn
    row_words = hp.shape[1]
    rows_per_step = tm // nf1

    def in_chunk(i, s, nv):
        return jnp.where(i < nv[0], jnp.minimum(s, nf1 - 1), nf1 - 1)

    def out_chunk(i, s, nv):
        return jnp.where(i < nv[0], jnp.maximum(s - nf1, 0), nf2 - 1)

    def wo_idx(i, s, te, nv):
        hold = jnp.logical_and(s == 0, i > 0)
        e = jnp.where(hold, te[jnp.maximum(i - 1, 0)], te[i])
        return (layer, e, 0, jnp.where(hold, nf2 - 1, out_chunk(i, s, nv)))

    wblk = (None, None, d, tf)
    bblk = (None, None, 1, tf)
    grid_spec = pltpu.PrefetchScalarGridSpec(
        num_scalar_prefetch=2,
        grid=(cap // tm, nf1 + nf2),
        in_specs=[
            pl.BlockSpec(memory_space=pl.ANY),
            pl.BlockSpec(memory_space=pl.ANY),
            pl.BlockSpec(wblk, lambda i, s, te, nv: (layer, te[i], 0, in_chunk(i, s, nv))),
            pl.BlockSpec(wblk, lambda i, s, te, nv: (layer, te[i], 0, nf1 + in_chunk(i, s, nv))),
            pl.BlockSpec(bblk, lambda i, s, te, nv: (layer, te[i], 0, in_chunk(i, s, nv))),
            pl.BlockSpec(bblk, lambda i, s, te, nv: (layer, te[i], 0, nf1 + in_chunk(i, s, nv))),
            pl.BlockSpec((None, None, d_ff, tn), wo_idx),
            pl.BlockSpec((None, None, 1, tn),
                         lambda i, s, te, nv: (layer, te[i], 0, out_chunk(i, s, nv))),
        ],
        out_specs=pl.BlockSpec((tm, tn), lambda i, s, te, nv: (i, jnp.maximum(s - nf1, 0))),
        scratch_shapes=[pltpu.SMEM((2, 1, tm), I32), pltpu.VMEM((2, tm, row_words), U32),
                        pltpu.VMEM((tm, d), BF16), pltpu.VMEM((nf1, tm, tf), BF16),
                        pltpu.SemaphoreType.DMA, pltpu.SemaphoreType.DMA((2,))],
    )
    b_in4 = b_in.reshape(depth, n_exp, 1, two_ff)
    return pl.pallas_call(
        functools.partial(_moe_kernel, tm=tm, nf1=nf1, rows_per_step=rows_per_step),
        out_shape=jax.ShapeDtypeStruct((cap, d), F32),
        grid_spec=grid_spec,
        compiler_params=_params(2),
        name=f"moe_experts_{layer}",
    )(tile_expert, n_valid, (slot_token * row_words).reshape(cap // tm, 1, tm), hp.reshape(-1),
      w_in, w_in, b_in4, b_in4, w_out, b_out.reshape(depth, n_exp, 1, d))


def _combine_kernel(slot_hbm, gate_ref, x_ref, gt_ref, y_hbm, o_ref, idx, buf, isem, sem, *, tc):
    i = pl.program_id(0)
    n = pl.num_programs(0)
    cur = i % 2
    nxt = 1 - cur

    def row_copy(b, r, k):
        return pltpu.make_async_copy(y_hbm.at[pl.ds(idx[b, 0, r * TOP_K + k], 1)],
                                     buf.at[b, k, pl.ds(r, 1)], sem.at[b])

    def start_tile(tile, b):
        cp = pltpu.make_async_copy(slot_hbm.at[tile], idx.at[b], isem)
        cp.start()
        cp.wait()

        def issue(r, c):
            for k in range(TOP_K):
                row_copy(b, r, k).start()
            return c

        lax.fori_loop(0, tc, issue, 0)

    @pl.when(i == 0)
    def _():
        start_tile(0, 0)

    @pl.when(i + 1 < n)
    def _():
        start_tile(i + 1, nxt)

    def drain(r, c):
        for k in range(TOP_K):
            row_copy(cur, r, k).wait()
        return c

    lax.fori_loop(0, tc, drain, 0)
    gate = gate_ref[...]
    f = gate[:, 0:1] * buf[cur, 0]
    for k in range(1, TOP_K):
        f = f + gate[:, k:k + 1] * buf[cur, k]
    o_ref[...] = x_ref[...] + gt_ref[...] * f


def _moe_combine(slot, gates, x1, mod, layer, y_sorted, *, t_ctx, n_lat):
    t, d = x1.shape
    tc = _pick(math.gcd(t_ctx, n_lat), (128,))
    cond_fn = functools.partial(_cond_of_tile, n_ctx_tiles=t_ctx // tc, tiles_per_seq=n_lat // tc)
    row = lambda i: (i, 0)
    return pl.pallas_call(
        functools.partial(_combine_kernel, tc=tc),
        out_shape=jax.ShapeDtypeStruct((t, d), F32),
        grid=(t // tc,),
        in_specs=[
            pl.BlockSpec(memory_space=pl.ANY),
            pl.BlockSpec((tc, LANE), row),
            pl.BlockSpec((tc, d), row),
            _mod_spec(layer, 5, d, cond_fn),
            pl.BlockSpec(memory_space=pl.ANY),
        ],
        out_specs=pl.BlockSpec((tc, d), row),
        scratch_shapes=[pltpu.SMEM((2, 1, tc * TOP_K), I32), pltpu.VMEM((2, TOP_K, tc, d), F32),
                        pltpu.SemaphoreType.DMA, pltpu.SemaphoreType.DMA((2,))],
        compiler_params=_params(1),
        name="moe_combine",
    )(slot.reshape(t // tc, 1, tc * TOP_K), gates, x1, mod, y_sorted)


def _moe_tile(n_assign, n_exp):
    per_expert = n_assign // n_exp
    for tm in (1024, 512, 256):
        if per_expert >= 3 * tm:
            return tm
    return 128


def _moe_ffn(hp, route, gates, counts, x1, mod, layer, w_in, b_in, w_out, b_out, *, t_ctx, n_lat):
    t = x1.shape[0]
    n_exp = w_in.shape[1]
    n_assign = t * TOP_K
    tm = _moe_tile(n_assign, n_exp)
    cap = -(-(n_assign + n_exp * (tm - 1)) // tm) * tm
    n_tiles = cap // tm

    counts = counts.reshape(n_exp).astype(I32)
    padded = (counts + tm - 1) // tm * tm
    pend = jnp.cumsum(padded)
    pstart = pend - padded
    idx = route[:, :TOP_K]
    slot = pstart[idx] + route[:, TOP_K:2 * TOP_K]
    token_of = jnp.broadcast_to(jnp.arange(t, dtype=I32)[:, None], (t, TOP_K))
    slot_token = jnp.zeros((cap,), I32).at[slot.reshape(-1)].set(
        token_of.reshape(-1), unique_indices=True)
    n_valid = (pend[-1] // tm).astype(I32)
    tile_start = jnp.minimum(jnp.arange(n_tiles, dtype=I32), n_valid - 1) * tm
    tile_expert = jnp.minimum(jnp.searchsorted(pend, tile_start, side='right'),
                              n_exp - 1).astype(I32)

    y_sorted = _moe_experts(hp, slot_token, tile_expert, n_valid.reshape(1), layer, w_in, b_in,
                            w_out, b_out, tm=tm)
    return _moe_combine(slot, gates, x1, mod, layer, y_sorted, t_ctx=t_ctx, n_lat=n_lat)


def _head_table(gain, n_heads, scale=1.0):
    return jnp.tile(gain * scale, n_heads)


def kernel(x_prompt, x_sample, cache_a_k, cache_a_v, cache_b_k, cache_b_v, c, c_ctx, w_ada, b_ada, g_mix, g_ffn, w_qkv_a, q_norm_a, k_norm_a, lam_q1_a, lam_k1_a, lam_q2_a, lam_k2_a, subln_a, w_o_a, w_qkv_b, q_norm_b, k_norm_b, sinks_b, w_o_b, w_router, b_router, w_in, b_in, w_out, b_out):
    batch, seq, d = x_prompt.shape
    dec_batch, n_lat, _ = x_sample.shape
    depth = w_ada.shape[0]
    past = cache_a_k.shape[2]
    heads_a = cache_a_k.shape[3]
    kv_heads_b = cache_b_k.shape[3]
    q_heads_b = sinks_b.shape[1]
    t_ctx = batch * seq
    t_lat = dec_batch * n_lat
    assert dec_batch + 1 <= N_COND

    x = jnp.concatenate([x_prompt.reshape(t_ctx, d), x_sample.reshape(t_lat, d)], axis=0)
    cond = jnp.concatenate([c_ctx[None, :], c, jnp.zeros((N_COND - 1 - dec_batch, d), F32)], axis=0)
    mod = _ada_params(cond, w_ada, b_ada).reshape(depth, N_COND, 6, 1, d)

    tm_qkv = _pick(math.gcd(t_ctx, n_lat), (1024, 512, 256, 128))
    cos_tab, sin_tab = _rope_tables(n_lat, tm_qkv)
    grp = jnp.where((jnp.arange(LANE)[:, None] // HEAD) == (jnp.arange(LANE)[None, :] // HEAD),
                    1.0 / HEAD, 0.0).astype(BF16)

    wa = heads_a * LANE
    qw_b = q_heads_b * HEAD
    kvw_b = kv_heads_b * HEAD
    states = {"a_k": [], "a_v": [], "b_k": [], "b_v": []}

    for layer in range(depth):
        i = layer // 2
        if layer % 2 == 0:
            lambda_init = 0.8 - 0.6 * math.exp(-0.3 * layer)
            gain = jnp.concatenate([
                _head_table(q_norm_a[i], 2 * heads_a, HEAD ** -0.5),
                _head_table(k_norm_a[i], 2 * heads_a),
                jnp.ones((wa,), F32)]).reshape(1, 3 * wa)
            qkv = _qkv_project(x, mod, layer, g_mix[layer], w_qkv_a[i].astype(BF16), gain,
                               cos_tab, sin_tab, grp, tm=tm_qkv, n_norm=2 * wa, t_ctx=t_ctx, n_lat=n_lat)
            states["a_k"].append(qkv[:t_ctx, wa:2 * wa].reshape(batch, seq, heads_a, 2, HEAD))
            states["a_v"].append(qkv[:t_ctx, 2 * wa:].reshape(batch, seq, heads_a, LANE))
            lamp = jnp.stack([lam_q1_a[i], lam_k1_a[i], lam_q2_a[i], lam_k2_a[i]])
            o_p = _diff_attention(qkv, lamp, subln_a[i], None, None, row0=0, n_batch=batch,
                                  n_own=seq, heads_total=heads_a, lambda_init=lambda_init)
            o_s = _diff_attention(qkv, lamp, subln_a[i],
                                  cache_a_k[:, i].reshape(dec_batch * past, wa),
                                  cache_a_v[:, i].reshape(dec_batch * past, wa),
                                  row0=t_ctx, n_batch=dec_batch, n_own=n_lat, heads_total=heads_a,
                                  lambda_init=lambda_init)
            w_o = w_o_a[i]
        else:
            wq, wk, wv = jnp.split(w_qkv_b[i], [qw_b, qw_b + kvw_b], axis=1)
            dup = lambda w: jnp.concatenate([w.reshape(d, kv_heads_b, 1, HEAD)] * 2, axis=2).reshape(
                d, 2 * kvw_b)
            w_b = jnp.concatenate([wq, dup(wk), dup(wv)], axis=1).astype(BF16)
            gain = jnp.concatenate([
                _head_table(q_norm_b[i], q_heads_b, HEAD ** -0.5),
                _head_table(k_norm_b[i], 2 * kv_heads_b),
                jnp.ones((2 * kvw_b,), F32)]).reshape(1, qw_b + 4 * kvw_b)
            qkv = _qkv_project(x, mod, layer, g_mix[layer], w_b, gain, cos_tab, sin_tab, grp,
                               tm=tm_qkv, n_norm=qw_b + 2 * kvw_b, t_ctx=t_ctx, n_lat=n_lat)
            k_state = qkv[:t_ctx, qw_b:qw_b + 2 * kvw_b].reshape(batch, seq, kv_heads_b, 2, HEAD)
            v_state = qkv[:t_ctx, qw_b + 2 * kvw_b:].reshape(batch, seq, kv_heads_b, 2, HEAD)
            states["b_k"].append(k_state[:, :, :, 0])
            states["b_v"].append(v_state[:, :, :, 0])
            dup_c = lambda a: jnp.concatenate([a, a], axis=-1).reshape(dec_batch * past,
                                                                       2 * kvw_b)
            o_p = _sink_attention(qkv, sinks_b[i], None, None, row0=0, n_batch=batch, n_own=seq,
                                  q_width=qw_b, kv_heads=kv_heads_b, windowed=False)
            o_s = _sink_attention(qkv, sinks_b[i], dup_c(cache_b_k[:, i]), dup_c(cache_b_v[:, i]),
                                  row0=t_ctx, n_batch=dec_batch, n_own=n_lat, q_width=qw_b,
                                  kv_heads=kv_heads_b, windowed=True)
            w_o = w_o_b[i]

        x1, hp, route, gates, counts = _out_project(
            o_p, o_s, w_o.astype(BF16), x, mod, layer, g_ffn[layer], w_router[layer],
            b_router[layer], t_ctx=t_ctx, n_lat=n_lat)
        x = _moe_ffn(hp, route, gates, counts, x1, mod, layer, w_in, b_in, w_out, b_out,
                     t_ctx=t_ctx, n_lat=n_lat)

    return (x[:t_ctx].reshape(batch, seq, d), x[t_ctx:].reshape(dec_batch, n_lat, d),
            jnp.stack(states["a_k"], axis=1), jnp.stack(states["a_v"], axis=1),
            jnp.stack(states["b_k"], axis=1), jnp.stack(states["b_v"], axis=1))
```

```python
import functools
import math

import jax
import jax.numpy as jnp
from jax import lax
from jax.experimental import pallas as pl
from jax.experimental.pallas import tpu as pltpu

F32 = jnp.float32
BF16 = jnp.bfloat16
I32 = jnp.int32
U32 = jnp.uint32

LANE = 128
HEAD = 64
NORM_EPS = 1e-6
ROPE_BASE = 10000.0
GRID_W = 64
WINDOW = 128
TOP_K = 4
SWIGLU_LIMIT = 7.0
SWIGLU_ALPHA = 1.702
NEG_INF = -1e30
LOG2E = 1.4426950408889634
N_COND = 8
VMEM_LIMIT = 56 * 1024 * 1024
GATHER_PRIORITY = 1


def _params(n_axes):
    return pltpu.CompilerParams(
        dimension_semantics=("arbitrary",) * n_axes, vmem_limit_bytes=VMEM_LIMIT)


def _pick(n, prefs):
    for p in prefs:
        if n % p == 0:
            return p
    raise ValueError(f"no tile in {prefs} divides {n}")


def _dot(a, b):
    return jnp.dot(a, b, preferred_element_type=F32)


def _dot_nt(a, b):
    return lax.dot_general(a, b, (((1,), (1,)), ((), ())), preferred_element_type=F32)


def _split(v):
    hi = v.astype(BF16)
    lo = (v - hi.astype(F32)).astype(BF16)
    return hi, lo


def _dot3(a, b):
    a_hi, a_lo = _split(a)
    b_hi, b_lo = _split(b)
    return _dot(a_hi, b_hi) + _dot(a_lo, b_hi) + _dot(a_hi, b_lo)


def _ada_kernel(c_ref, w_ref, b_ref, o_ref):
    c = c_ref[...]
    s = c * jax.nn.sigmoid(c)
    o_ref[...] = _dot3(s, w_ref[...]) + b_ref[...]


def _ada_params(cond, w_ada, b_ada):
    depth, d, n = w_ada.shape
    tn = _pick(n, (512, 256, 128))
    return pl.pallas_call(
        _ada_kernel,
        out_shape=jax.ShapeDtypeStruct((depth, N_COND, n), F32),
        grid=(depth, n // tn),
        in_specs=[
            pl.BlockSpec((N_COND, d), lambda l, j: (0, 0)),
            pl.BlockSpec((None, d, tn), lambda l, j: (l, 0, j)),
            pl.BlockSpec((None, 1, tn), lambda l, j: (l, 0, j)),
        ],
        out_specs=pl.BlockSpec((None, N_COND, tn), lambda l, j: (l, 0, j)),
        compiler_params=_params(2),
        name="ada_params",
    )(cond, w_ada, b_ada.reshape(depth, 1, n))


def _modulate(x, g, sh, sc):
    xn = x * lax.rsqrt(jnp.mean(x * x, axis=-1, keepdims=True) + NORM_EPS)
    return xn * g * (1.0 + sc) + sh


def _qkv_kernel(x_ref, sh_ref, sc_ref, g_ref, w_ref, gain_ref, cos_ref, sin_ref, grp_ref,
                o_ref, h_scr, *, n_norm_tiles, tn):
    j = pl.program_id(1)

    @pl.when(j == 0)
    def _():
        h_scr[...] = _modulate(x_ref[...], g_ref[...], sh_ref[...], sc_ref[...]).astype(BF16)

    y = _dot(h_scr[...], w_ref[...])

    @pl.when(j < n_norm_tiles)
    def _():
        lane = lax.broadcasted_iota(I32, (1, LANE), 1)
        first = (lane & 16) == 0
        cos = cos_ref[...]
        sin = sin_ref[...]
        for s in range(tn // LANE):
            cols = slice(s * LANE, (s + 1) * LANE)
            ys = y[:, cols]
            ms = _dot((ys * ys).astype(BF16), grp_ref[...])
            yn = ys * lax.rsqrt(ms + NORM_EPS) * gain_ref[:, cols]
            rot = jnp.where(first, pltpu.roll(yn, LANE - 16, 1), pltpu.roll(yn, 16, 1))
            o_ref[:, cols] = yn * cos + rot * sin

    @pl.when(j >= n_norm_tiles)
    def _():
        o_ref[...] = y


def _cond_of_tile(i, n_ctx_tiles, tiles_per_seq):
    return jnp.where(i < n_ctx_tiles, 0, 1 + (i - n_ctx_tiles) // tiles_per_seq)


def _mod_spec(layer, which, d, cond_fn):
    return pl.BlockSpec((None, None, None, 1, d),
                        lambda i, *_: (layer, cond_fn(i), which, 0, 0))


def _qkv_project(x, mod, layer, g_mix, w, gain, cos_tab, sin_tab, grp, *, tm, n_norm, t_ctx, n_lat):
    t, d = x.shape
    n = w.shape[1]
    tn = _pick(math.gcd(n_norm, n - n_norm), (512, 256, 128))
    n_ctx_tiles = t_ctx // tm
    tiles_per_seq = n_lat // tm
    cond_fn = functools.partial(_cond_of_tile, n_ctx_tiles=n_ctx_tiles, tiles_per_seq=tiles_per_seq)

    def rope_idx(i, j):
        return (jnp.where(i < n_ctx_tiles, 0, 1 + (i - n_ctx_tiles) % tiles_per_seq), 0)

    return pl.pallas_call(
        functools.partial(_qkv_kernel, n_norm_tiles=n_norm // tn, tn=tn),
        out_shape=jax.ShapeDtypeStruct((t, n), F32),
        grid=(t // tm, n // tn),
        in_specs=[
            pl.BlockSpec((tm, d), lambda i, j: (i, 0)),
            _mod_spec(layer, 0, d, cond_fn),
            _mod_spec(layer, 1, d, cond_fn),
            pl.BlockSpec((1, d), lambda i, j: (0, 0)),
            pl.BlockSpec((d, tn), lambda i, j: (0, j)),
            pl.BlockSpec((1, tn), lambda i, j: (0, j)),
            pl.BlockSpec((tm, LANE), rope_idx),
            pl.BlockSpec((tm, LANE), rope_idx),
            pl.BlockSpec((LANE, LANE), lambda i, j: (0, 0)),
        ],
        out_specs=pl.BlockSpec((tm, tn), lambda i, j: (i, j)),
        scratch_shapes=[pltpu.VMEM((tm, d), BF16)],
        compiler_params=_params(2),
        name=f"qkv_project_{layer}",
    )(x, mod, mod, g_mix.reshape(1, d), w, gain, cos_tab, sin_tab, grp)


def _rope_tables(n_lat, tm):
    rows = n_lat // GRID_W
    row = jnp.broadcast_to(jnp.arange(rows, dtype=F32)[:, None], (rows, GRID_W)).reshape(-1)
    col = jnp.broadcast_to(jnp.arange(GRID_W, dtype=F32)[None, :], (rows, GRID_W)).reshape(-1)
    quarter = HEAD // 4
    inv = ROPE_BASE ** (-jnp.arange(quarter, dtype=F32) / quarter)
    ang_r = row[:, None] * inv[None, :]
    ang_c = col[:, None] * inv[None, :]
    ang = jnp.concatenate([ang_r, ang_r, ang_c, ang_c], axis=-1)
    sign = jnp.where((jnp.arange(HEAD) % 32) < 16, -1.0, 1.0).astype(F32)
    cos = jnp.cos(ang)
    sin = jnp.sin(ang) * sign
    cos = jnp.concatenate([cos, cos], axis=-1)
    sin = jnp.concatenate([sin, sin], axis=-1)
    cos = jnp.concatenate([jnp.ones((tm, LANE), F32), cos], axis=0)
    sin = jnp.concatenate([jnp.zeros((tm, LANE), F32), sin], axis=0)
    return cos, sin


def _attn_a_kernel(lamp_ref, sub_ref, q_ref, *rest, n_ctx, n_own, tq, tk, heads, lambda_init):
    if n_ctx:
        ck_ref, cv_ref, k_ref, v_ref, o_ref, kb, vb, acc = rest
    else:
        k_ref, v_ref, o_ref, kb, vb, acc = rest
    qi = pl.program_id(2)
    n_keys = n_own + n_ctx

    @pl.when(qi == 0)
    def _():
        kb[0:n_own, :] = k_ref[...].astype(BF16)
        vb[0:n_own, :] = v_ref[...].astype(BF16)
        if n_ctx:
            kb[n_own:n_keys, :] = ck_ref[...].astype(BF16)
            vb[n_own:n_keys, :] = cv_ref[...].astype(BF16)

    lp = lamp_ref[...]
    lam = (jnp.exp(jnp.sum(lp[0:1] * lp[1:2], axis=-1, keepdims=True))
           - jnp.exp(jnp.sum(lp[2:3] * lp[3:4], axis=-1, keepdims=True)) + lambda_init)
    lo = lax.broadcasted_iota(I32, (1, LANE), 1) < HEAD
    n_main = n_own // tk
    tail = n_keys - n_main * tk

    cols = [slice(h * LANE, (h + 1) * LANE) for h in range(heads)]
    q2 = []
    for h in range(heads):
        q = q_ref[:, cols[h]] * LOG2E
        q2.append(jnp.concatenate([jnp.where(lo, q, 0.0), jnp.where(lo, 0.0, q)],
                                  axis=0).astype(BF16))
    acc[...] = jnp.zeros_like(acc)

    def step(rows, carry):
        out = []
        for h in range(heads):
            m, l = carry[h]
            s = _dot_nt(q2[h], kb[rows, cols[h]])
            m_new = jnp.maximum(m, jnp.max(s, axis=-1, keepdims=True))
            alpha = jnp.exp2(m - m_new)
            p = jnp.exp2(s - m_new)
            acc[h] = alpha * acc[h] + _dot(p.astype(BF16), vb[rows, cols[h]])
            out.append((m_new, alpha * l + jnp.sum(p, axis=-1, keepdims=True)))
        return tuple(out)

    def body(j, carry):
        return step(pl.ds(pl.multiple_of(j * tk, tk), tk), carry)

    carry = tuple((jnp.full((2 * tq, 1), NEG_INF, F32), jnp.zeros((2 * tq, 1), F32))
                  for _ in range(heads))
    if n_main:
        carry = lax.fori_loop(0, n_main, body, carry)
    if tail:
        carry = step(slice(n_main * tk, n_keys), carry)
    for h in range(heads):
        on = acc[h] / carry[h][1]
        o = on[0:tq] - lam * on[tq:2 * tq]
        o = o * lax.rsqrt(jnp.mean(o * o, axis=-1, keepdims=True) + NORM_EPS)
        o_ref[:, cols[h]] = (o * sub_ref[...] * (1.0 - lambda_init)).astype(BF16)


def _diff_attention(qkv, lamp, sub_gain, cache_k, cache_v, *, row0, n_batch, n_own, heads_total,
                    lambda_init):
    width = heads_total * LANE
    heads = 2 if heads_total % 2 == 0 else 1
    gw = heads * LANE
    n_groups = heads_total // heads
    tq = _pick(n_own, (256, 128))
    tk = _pick(n_own, (2048, 1024, 512, 256, 128))
    n_ctx = 0 if cache_k is None else cache_k.shape[0] // n_batch
    q_tiles = n_own // tq
    q_row0 = row0 // tq
    kv_row0 = row0 // n_own
    sec = width // gw

    in_specs = [
        pl.BlockSpec((4, HEAD), lambda b, g, i: (0, 0)),
        pl.BlockSpec((1, LANE), lambda b, g, i: (0, 0)),
        pl.BlockSpec((tq, gw), lambda b, g, i: (q_row0 + b * q_tiles + i, g)),
    ]
    args = [lamp, sub_gain.reshape(1, LANE), qkv]
    if n_ctx:
        in_specs += [pl.BlockSpec((n_ctx, gw), lambda b, g, i: (b, g)),
                     pl.BlockSpec((n_ctx, gw), lambda b, g, i: (b, g))]
        args += [cache_k, cache_v]
    in_specs += [pl.BlockSpec((n_own, gw), lambda b, g, i: (kv_row0 + b, sec + g)),
                 pl.BlockSpec((n_own, gw), lambda b, g, i: (kv_row0 + b, 2 * sec + g))]
    args += [qkv, qkv]
    n_keys = n_own + n_ctx
    return pl.pallas_call(
        functools.partial(_attn_a_kernel, n_ctx=n_ctx, n_own=n_own, tq=tq, tk=tk, heads=heads,
                          lambda_init=lambda_init),
        out_shape=jax.ShapeDtypeStruct((n_batch * n_own, width), BF16),
        grid=(n_batch, n_groups, q_tiles),
        in_specs=in_specs,
        out_specs=pl.BlockSpec((tq, gw), lambda b, g, i: (b * q_tiles + i, g)),
        scratch_shapes=[pltpu.VMEM((n_keys, gw), BF16), pltpu.VMEM((n_keys, gw), BF16),
                        pltpu.VMEM((heads, 2 * tq, LANE), F32)],
        compiler_params=_params(3),
        name="diff_attention_ctx" if n_ctx else "diff_attention_self",
    )(*args)


def _attn_b_kernel(sink_ref, q_ref, *rest, n_ctx, n_own, tq, windowed, slabs):
    if n_ctx:
        ck_ref, cv_ref, k_ref, v_ref, o_ref, kb, vb = rest
    else:
        k_ref, v_ref, o_ref, kb, vb = rest
    g = pl.program_id(1)
    qi = pl.program_id(2)
    pad = WINDOW if windowed else 0
    own0 = n_ctx + pad

    @pl.when(qi == 0)
    def _():
        if n_ctx:
            kb[0:n_ctx, :] = ck_ref[...].astype(BF16)
            vb[0:n_ctx, :] = cv_ref[...].astype(BF16)
        if pad:
            zeros = jnp.zeros((pad, LANE), BF16)
            for ref in (kb, vb):
                ref[n_ctx:own0, :] = zeros
                ref[own0 + n_own:own0 + n_own + pad, :] = zeros
        kb[own0:own0 + n_own, :] = k_ref[...].astype(BF16)
        vb[own0:own0 + n_own, :] = v_ref[...].astype(BF16)

    if windowed:
        wlen = tq + 2 * pad
        st = pl.multiple_of(n_ctx + qi * tq, tq)
        k_w = kb[pl.ds(st, wlen), :]
        v_w = vb[pl.ds(st, wlen), :]
        r = lax.broadcasted_iota(I32, (2 * tq, wlen), 0) & (tq - 1)
        c = lax.broadcasted_iota(I32, (2 * tq, wlen), 1)
        kpos = qi * tq - pad + c
        d = c - r
        mask = (d >= 0) & (d <= 2 * pad) & (kpos >= 0) & (kpos < n_own)
    else:
        k_w = kb[own0:own0 + n_own, :]
        v_w = vb[own0:own0 + n_own, :]
        mask = None
    if n_ctx:
        k_c = kb[0:n_ctx, :]
        v_c = vb[0:n_ctx, :]

    lo = lax.broadcasted_iota(I32, (1, LANE), 1) < HEAD
    top = lax.broadcasted_iota(I32, (2 * tq, 1), 0) < tq
    for t in range(slabs):
        cols = slice(t * LANE, (t + 1) * LANE)
        q = q_ref[:, cols] * LOG2E
        q2 = jnp.concatenate([jnp.where(lo, q, 0.0), jnp.where(lo, 0.0, q)], axis=0).astype(BF16)
        head0 = g * (2 * slabs) + 2 * t
        sk = jnp.where(top, sink_ref[head0], sink_ref[head0 + 1]) * LOG2E
        s_w = _dot_nt(q2, k_w)
        if mask is not None:
            s_w = jnp.where(mask, s_w, NEG_INF)
        m = jnp.maximum(jnp.max(s_w, axis=-1, keepdims=True), sk)
        if n_ctx:
            s_c = _dot_nt(q2, k_c)
            m = jnp.maximum(m, jnp.max(s_c, axis=-1, keepdims=True))
        p_w = jnp.exp2(s_w - m)
        den = jnp.sum(p_w, axis=-1, keepdims=True) + jnp.exp2(sk - m)
        o = _dot(p_w.astype(BF16), v_w)
        if n_ctx:
            p_c = jnp.exp2(s_c - m)
            den = den + jnp.sum(p_c, axis=-1, keepdims=True)
            o = o + _dot(p_c.astype(BF16), v_c)
        o = o / den
        o_ref[:, cols] = jnp.where(lo, o[0:tq], o[tq:2 * tq]).astype(BF16)


def _sink_attention(qkv, sinks, cache_k, cache_v, *, row0, n_batch, n_own, q_width, kv_heads,
                    windowed):
    gw = q_width // kv_heads
    slabs = gw // LANE
    tq = _pick(n_own, (256, 128))
    n_ctx = 0 if cache_k is None else cache_k.shape[0] // n_batch
    q_tiles = n_own // tq
    q_row0 = row0 // tq
    kv_row0 = row0 // n_own
    k_col0 = q_width // LANE
    v_col0 = k_col0 + kv_heads
    pad = WINDOW if windowed else 0

    in_specs = [
        pl.BlockSpec(memory_space=pltpu.SMEM),
        pl.BlockSpec((tq, gw), lambda b, g, i: (q_row0 + b * q_tiles + i, g)),
    ]
    args = [sinks, qkv]
    if n_ctx:
        in_specs += [pl.BlockSpec((n_ctx, LANE), lambda b, g, i: (b, g)),
                     pl.BlockSpec((n_ctx, LANE), lambda b, g, i: (b, g))]
        args += [cache_k, cache_v]
    in_specs += [pl.BlockSpec((n_own, LANE), lambda b, g, i: (kv_row0 + b, k_col0 + g)),
                 pl.BlockSpec((n_own, LANE), lambda b, g, i: (kv_row0 + b, v_col0 + g))]
    args += [qkv, qkv]
    n_rows = n_ctx + n_own + 2 * pad
    return pl.pallas_call(
        functools.partial(_attn_b_kernel, n_ctx=n_ctx, n_own=n_own, tq=tq, windowed=windowed,
                          slabs=slabs),
        out_shape=jax.ShapeDtypeStruct((n_batch * n_own, q_width), BF16),
        grid=(n_batch, kv_heads, q_tiles),
        in_specs=in_specs,
        out_specs=pl.BlockSpec((tq, gw), lambda b, g, i: (b * q_tiles + i, g)),
        scratch_shapes=[pltpu.VMEM((n_rows, LANE), BF16), pltpu.VMEM((n_rows, LANE), BF16)],
        compiler_params=_params(3),
        name="sink_attention_window" if windowed else "sink_attention_self",
    )(*args)


def _out_kernel(oc_ref, ol_ref, wo_ref, x_ref, gt_ref, g_ref, sh_ref, sc_ref, wr_ref, br_ref,
                x1_ref, hp_ref, route_ref, gate_ref, cnt_ref, carry, *, n_exp, tm, n_ctx_tiles):
    i = pl.program_id(0)

    @pl.when(i == 0)
    def _():
        carry[...] = jnp.zeros_like(carry)

    o = jnp.where(i < n_ctx_tiles, oc_ref[...], ol_ref[...])
    x1 = x_ref[...] + gt_ref[...] * _dot(o, wo_ref[...])
    x1_ref[...] = x1
    h = _modulate(x1, g_ref[...], sh_ref[...], sc_ref[...])
    h_hi, h_lo = _split(h)
    half = h.shape[1] // 2
    ua = pltpu.bitcast(h_hi[:, :half].astype(F32), U32)
    ub = pltpu.bitcast(h_hi[:, half:].astype(F32), U32)
    hp_ref[...] = ua | (ub >> 16)

    w_hi, w_lo = _split(wr_ref[...])
    logits = _dot(h_hi, w_hi) + _dot(h_lo, w_hi) + _dot(h_hi, w_lo) + br_ref[...]

    eio = lax.broadcasted_iota(I32, (tm, n_exp), 1).astype(F32)
    work = logits
    tops, idxs, hots = [], [], []
    for _ in range(TOP_K):
        mk = jnp.max(work, axis=-1, keepdims=True)
        ik = jnp.min(jnp.where(work == mk, eio, float(n_exp)), axis=-1, keepdims=True)
        hot = eio == ik
        tops.append(mk)
        idxs.append(ik)
        hots.append(hot)
        work = jnp.where(hot, -3e38, work)
    es = [jnp.exp(t - tops[0]) for t in tops]
    den = es[0] + es[1] + es[2] + es[3]

    hot_all = jnp.zeros((tm, n_exp), F32)
    for hot in hots:
        hot_all = hot_all + jnp.where(hot, 1.0, 0.0)
    tri = jnp.where(lax.broadcasted_iota(I32, (tm, tm), 0) > lax.broadcasted_iota(I32, (tm, tm), 1),
                    1.0, 0.0).astype(BF16)
    base = _dot(tri, hot_all.astype(BF16)) + carry[...]
    ranks = [jnp.sum(jnp.where(hot, base, 0.0), axis=-1, keepdims=True).astype(I32) for hot in hots]
    carry[...] = carry[...] + jnp.sum(hot_all, axis=0, keepdims=True)
    cnt_ref[...] = carry[...]

    col = lax.broadcasted_iota(I32, (tm, LANE), 1)
    route = jnp.zeros((tm, LANE), I32)
    gate = jnp.zeros((tm, LANE), F32)
    for k in range(TOP_K):
        route = jnp.where(col == k, idxs[k].astype(I32), route)
        route = jnp.where(col == TOP_K + k, ranks[k], route)
        gate = jnp.where(col == k, es[k] / den, gate)
    route_ref[...] = route
    gate_ref[...] = gate


def _out_project(o_ctx, o_lat, w_o, x, mod, layer, g_ffn, w_router, b_router, *, t_ctx, n_lat):
    t, d = x.shape
    kdim = o_ctx.shape[1]
    n_exp = w_router.shape[1]
    tm = _pick(math.gcd(t_ctx, n_lat), (512, 256, 128))
    n_ctx_tiles = t_ctx // tm
    cond_fn = functools.partial(_cond_of_tile, n_ctx_tiles=n_ctx_tiles, tiles_per_seq=n_lat // tm)
    row = lambda i: (i, 0)
    fixed = lambda i: (0, 0)
    return pl.pallas_call(
        functools.partial(_out_kernel, n_exp=n_exp, tm=tm, n_ctx_tiles=n_ctx_tiles),
        out_shape=(jax.ShapeDtypeStruct((t, d), F32), jax.ShapeDtypeStruct((t, d // 2), U32),
                   jax.ShapeDtypeStruct((t, LANE), I32), jax.ShapeDtypeStruct((t, LANE), F32),
                   jax.ShapeDtypeStruct((1, n_exp), F32)),
        grid=(t // tm,),
        in_specs=[
            pl.BlockSpec((tm, kdim), lambda i: (jnp.minimum(i, n_ctx_tiles - 1), 0)),
            pl.BlockSpec((tm, kdim), lambda i: (jnp.maximum(i - n_ctx_tiles, 0), 0)),
            pl.BlockSpec((kdim, d), fixed),
            pl.BlockSpec((tm, d), row),
            _mod_spec(layer, 2, d, cond_fn),
            pl.BlockSpec((1, d), fixed),
            _mod_spec(layer, 3, d, cond_fn),
            _mod_spec(layer, 4, d, cond_fn),
            pl.BlockSpec((d, n_exp), fixed),
            pl.BlockSpec((1, n_exp), fixed),
        ],
        out_specs=(pl.BlockSpec((tm, d), row), pl.BlockSpec((tm, d // 2), row),
                   pl.BlockSpec((tm, LANE), row), pl.BlockSpec((tm, LANE), row),
                   pl.BlockSpec((1, n_exp), fixed)),
        scratch_shapes=[pltpu.VMEM((1, n_exp), F32)],
        compiler_params=_params(1),
        name=f"out_project_route_{layer}",
    )(o_ctx, o_lat, w_o, x, mod, g_ffn.reshape(1, d), mod, mod, w_router,
      b_router.reshape(1, n_exp))


def _moe_kernel(te_ref, nv_ref, tok_hbm, hp_hbm, wg_ref, wu_ref, bg_ref, bu_ref, wo_ref, bo_ref,
                y_ref, idx, xs, xb, act, isem, gsem, *, tm, nf1, rows_per_step):
    i = pl.program_id(0)
    s = pl.program_id(1)
    nv = nv_ref[0]
    cur = i % 2
    nxt = 1 - cur

    def fetch_idx(tile, buf):
        cp = pltpu.make_async_copy(tok_hbm.at[tile], idx.at[buf], isem)
        cp.start()
        cp.wait()

    row_words = xs.shape[2]

    def row_copy(buf, r):
        start = pl.multiple_of(idx[buf, 0, r], row_words)
        return pltpu.make_async_copy(hp_hbm.at[pl.ds(start, row_words)], xs.at[buf, r], gsem.at[buf])

    @pl.when(jnp.logical_and(i == 0, s == 0))
    def _():
        fetch_idx(0, 0)

        def issue(r, c):
            row_copy(0, r).start(priority=GATHER_PRIORITY)
            return c

        lax.fori_loop(0, tm, issue, 0)

    @pl.when(jnp.logical_and(i < nv, s == 0))
    def _():
        def drain(r, c):
            row_copy(cur, r).wait()
            return c

        lax.fori_loop(0, tm, drain, 0)
        u = xs[cur]
        xb[:, :row_words] = pltpu.bitcast(u & jnp.uint32(0xFFFF0000), F32).astype(BF16)
        xb[:, row_words:] = pltpu.bitcast(u << 16, F32).astype(BF16)

    @pl.when(jnp.logical_and(i + 1 < nv, s == 0))
    def _():
        fetch_idx(i + 1, nxt)

    def activation_chunk():
        wg = wg_ref[...].astype(BF16)
        wu = wu_ref[...].astype(BF16)
        for r in range(2):
            rows = slice(r * (tm // 2), (r + 1) * (tm // 2))
            x = xb[rows, :]
            g = _dot(x, wg) + bg_ref[...]
            u = _dot(x, wu) + bu_ref[...]
            g = jnp.minimum(g, SWIGLU_LIMIT)
            u = jnp.clip(u, -SWIGLU_LIMIT, SWIGLU_LIMIT)
            act[s, rows, :] = (g * jax.nn.sigmoid(SWIGLU_ALPHA * g) * (u + 1.0)).astype(BF16)

    @pl.when(jnp.logical_and(i + 1 < nv, s < nf1))
    def _():
        r0 = s * rows_per_step
        for r in range(rows_per_step):
            row_copy(nxt, r0 + r).start(priority=GATHER_PRIORITY)
        activation_chunk()

    @pl.when(jnp.logical_and(i + 1 == nv, s < nf1))
    def _():
        activation_chunk()

    @pl.when(jnp.logical_and(i < nv, s >= nf1))
    def _():
        a = jnp.concatenate([act[c] for c in range(nf1)], axis=1)
        y_ref[...] = _dot(a, wo_ref[...].astype(BF16)) + bo_ref[...]

    @pl.when(jnp.logical_and(i >= nv, s >= nf1))
    def _():
        y_ref[...] = jnp.zeros_like(y_ref)


def _moe_experts(hp, slot_token, tile_expert, n_valid, layer, w_in, b_in, w_out, b_out, *, tm):
    cap = slot_token.shape[0]
    depth, n_exp, d, two_ff = w_in.shape
    d_ff = two_ff // 2
    tf = _pick(d_ff, (256, 128))
    nf1 = d_ff // tf
    tn = _pick(d, (512, 256, 128))
    nf2 = d // tn
    row_words = hp.shape[1]
    rows_per_step = tm // nf1

    def in_chunk(i, s, nv):
        return jnp.where(i < nv[0], jnp.minimum(s, nf1 - 1), nf1 - 1)

    def out_chunk(i, s, nv):
        return jnp.where(i < nv[0], jnp.maximum(s - nf1, 0), nf2 - 1)

    def wo_idx(i, s, te, nv):
        hold = jnp.logical_and(s == 0, i > 0)
        e = jnp.where(hold, te[jnp.maximum(i - 1, 0)], te[i])
        return (layer, e, 0, jnp.where(hold, nf2 - 1, out_chunk(i, s, nv)))

    wblk = (None, None, d, tf)
    bblk = (None, None, 1, tf)
    grid_spec = pltpu.PrefetchScalarGridSpec(
        num_scalar_prefetch=2,
        grid=(cap // tm, nf1 + nf2),
        in_specs=[
            pl.BlockSpec(memory_space=pl.ANY),
            pl.BlockSpec(memory_space=pl.ANY),
            pl.BlockSpec(wblk, lambda i, s, te, nv: (layer, te[i], 0, in_chunk(i, s, nv))),
            pl.BlockSpec(wblk, lambda i, s, te, nv: (layer, te[i], 0, nf1 + in_chunk(i, s, nv))),
            pl.BlockSpec(bblk, lambda i, s, te, nv: (layer, te[i], 0, in_chunk(i, s, nv))),
            pl.BlockSpec(bblk, lambda i, s, te, nv: (layer, te[i], 0, nf1 + in_chunk(i, s, nv))),
            pl.BlockSpec((None, None, d_ff, tn), wo_idx),
            pl.BlockSpec((None, None, 1, tn),
                         lambda i, s, te, nv: (layer, te[i], 0, out_chunk(i, s, nv))),
        ],
        out_specs=pl.BlockSpec((tm, tn), lambda i, s, te, nv: (i, jnp.maximum(s - nf1, 0))),
        scratch_shapes=[pltpu.SMEM((2, 1, tm), I32), pltpu.VMEM((2, tm, row_words), U32),
                        pltpu.VMEM((tm, d), BF16), pltpu.VMEM((nf1, tm, tf), BF16),
                        pltpu.SemaphoreType.DMA, pltpu.SemaphoreType.DMA((2,))],
    )
    b_in4 = b_in.reshape(depth, n_exp, 1, two_ff)
    return pl.pallas_call(
        functools.partial(_moe_kernel, tm=tm, nf1=nf1, rows_per_step=rows_per_step),
        out_shape=jax.ShapeDtypeStruct((cap, d), F32),
        grid_spec=grid_spec,
        compiler_params=_params(2),
        name=f"moe_experts_{layer}",
    )(tile_expert, n_valid, (slot_token * row_words).reshape(cap // tm, 1, tm), hp.reshape(-1),
      w_in, w_in, b_in4, b_in4, w_out, b_out.reshape(depth, n_exp, 1, d))


def _combine_kernel(slot_hbm, gate_ref, x_ref, gt_ref, y_hbm, o_ref, idx, buf, isem, sem, *, tc):
    i = pl.program_id(0)
    n = pl.num_programs(0)
    cur = i % 2
    nxt = 1 - cur

    def row_copy(b, r, k):
        return pltpu.make_async_copy(y_hbm.at[pl.ds(idx[b, 0, r * TOP_K + k], 1)],
                                     buf.at[b, k, pl.ds(r, 1)], sem.at[b])

    def start_tile(tile, b):
        cp = pltpu.make_async_copy(slot_hbm.at[tile], idx.at[b], isem)
        cp.start()
        cp.wait()

        def issue(r, c):
            for k in range(TOP_K):
                row_copy(b, r, k).start(priority=k % 2)
            return c

        lax.fori_loop(0, tc, issue, 0)

    @pl.when(i == 0)
    def _():
        start_tile(0, 0)

    @pl.when(i + 1 < n)
    def _():
        start_tile(i + 1, nxt)

    def drain(r, c):
        for k in range(TOP_K):
            row_copy(cur, r, k).wait()
        return c

    lax.fori_loop(0, tc, drain, 0)
    gate = gate_ref[...]
    f = gate[:, 0:1] * buf[cur, 0]
    for k in range(1, TOP_K):
        f = f + gate[:, k:k + 1] * buf[cur, k]
    o_ref[...] = x_ref[...] + gt_ref[...] * f


def _moe_combine(slot, gates, x1, mod, layer, y_sorted, *, t_ctx, n_lat):
    t, d = x1.shape
    tc = _pick(math.gcd(t_ctx, n_lat), (128,))
    cond_fn = functools.partial(_cond_of_tile, n_ctx_tiles=t_ctx // tc, tiles_per_seq=n_lat // tc)
    row = lambda i: (i, 0)
    return pl.pallas_call(
        functools.partial(_combine_kernel, tc=tc),
        out_shape=jax.ShapeDtypeStruct((t, d), F32),
        grid=(t // tc,),
        in_specs=[
            pl.BlockSpec(memory_space=pl.ANY),
            pl.BlockSpec((tc, LANE), row),
            pl.BlockSpec((tc, d), row),
            _mod_spec(layer, 5, d, cond_fn),
            pl.BlockSpec(memory_space=pl.ANY),
        ],
        out_specs=pl.BlockSpec((tc, d), row),
        scratch_shapes=[pltpu.SMEM((2, 1, tc * TOP_K), I32), pltpu.VMEM((2, TOP_K, tc, d), F32),
                        pltpu.SemaphoreType.DMA, pltpu.SemaphoreType.DMA((2,))],
        compiler_params=_params(1),
        name="moe_combine",
    )(slot.reshape(t // tc, 1, tc * TOP_K), gates, x1, mod, y_sorted)


def _moe_tile(n_assign, n_exp):
    per_expert = n_assign // n_exp
    for tm in (1024, 512, 256):
        if per_expert >= 3 * tm:
            return tm
    return 128


def _moe_ffn(hp, route, gates, counts, x1, mod, layer, w_in, b_in, w_out, b_out, *, t_ctx, n_lat):
    t = x1.shape[0]
    n_exp = w_in.shape[1]
    n_assign = t * TOP_K
    tm = _moe_tile(n_assign, n_exp)
    cap = -(-(n_assign + n_exp * (tm - 1)) // tm) * tm
    n_tiles = cap // tm

    counts = counts.reshape(n_exp).astype(I32)
    padded = (counts + tm - 1) // tm * tm
    pend = jnp.cumsum(padded)
    pstart = pend - padded
    idx = route[:, :TOP_K]
    slot = pstart[idx] + route[:, TOP_K:2 * TOP_K]
    token_of = jnp.broadcast_to(jnp.arange(t, dtype=I32)[:, None], (t, TOP_K))
    slot_token = jnp.zeros((cap,), I32).at[slot.reshape(-1)].set(
        token_of.reshape(-1), unique_indices=True)
    n_valid = (pend[-1] // tm).astype(I32)
    tile_start = jnp.minimum(jnp.arange(n_tiles, dtype=I32), n_valid - 1) * tm
    tile_expert = jnp.minimum(jnp.searchsorted(pend, tile_start, side='right'),
                              n_exp - 1).astype(I32)

    y_sorted = _moe_experts(hp, slot_token, tile_expert, n_valid.reshape(1), layer, w_in, b_in,
                            w_out, b_out, tm=tm)
    return _moe_combine(slot, gates, x1, mod, layer, y_sorted, t_ctx=t_ctx, n_lat=n_lat)


def _head_table(gain, n_heads, scale=1.0):
    return jnp.tile(gain * scale, n_heads)


def kernel(x_prompt, x_sample, cache_a_k, cache_a_v, cache_b_k, cache_b_v, c, c_ctx, w_ada, b_ada, g_mix, g_ffn, w_qkv_a, q_norm_a, k_norm_a, lam_q1_a, lam_k1_a, lam_q2_a, lam_k2_a, subln_a, w_o_a, w_qkv_b, q_norm_b, k_norm_b, sinks_b, w_o_b, w_router, b_router, w_in, b_in, w_out, b_out):
    batch, seq, d = x_prompt.shape
    dec_batch, n_lat, _ = x_sample.shape
    depth = w_ada.shape[0]
    past = cache_a_k.shape[2]
    heads_a = cache_a_k.shape[3]
    kv_heads_b = cache_b_k.shape[3]
    q_heads_b = sinks_b.shape[1]
    t_ctx = batch * seq
    t_lat = dec_batch * n_lat
    assert dec_batch + 1 <= N_COND

    x = jnp.concatenate([x_prompt.reshape(t_ctx, d), x_sample.reshape(t_lat, d)], axis=0)
    cond = jnp.concatenate([c_ctx[None, :], c, jnp.zeros((N_COND - 1 - dec_batch, d), F32)], axis=0)
    mod = _ada_params(cond, w_ada, b_ada).reshape(depth, N_COND, 6, 1, d)

    tm_qkv = _pick(math.gcd(t_ctx, n_lat), (1024, 512, 256, 128))
    cos_tab, sin_tab = _rope_tables(n_lat, tm_qkv)
    grp = jnp.where((jnp.arange(LANE)[:, None] // HEAD) == (jnp.arange(LANE)[None, :] // HEAD),
                    1.0 / HEAD, 0.0).astype(BF16)

    wa = heads_a * LANE
    qw_b = q_heads_b * HEAD
    kvw_b = kv_heads_b * HEAD
    states = {"a_k": [], "a_v": [], "b_k": [], "b_v": []}

    for layer in range(depth):
        i = layer // 2
        if layer % 2 == 0:
            lambda_init = 0.8 - 0.6 * math.exp(-0.3 * layer)
            gain = jnp.concatenate([
                _head_table(q_norm_a[i], 2 * heads_a, HEAD ** -0.5),
                _head_table(k_norm_a[i], 2 * heads_a),
                jnp.ones((wa,), F32)]).reshape(1, 3 * wa)
            qkv = _qkv_project(x, mod, layer, g_mix[layer], w_qkv_a[i].astype(BF16), gain,
                               cos_tab, sin_tab, grp, tm=tm_qkv, n_norm=2 * wa, t_ctx=t_ctx, n_lat=n_lat)
            states["a_k"].append(qkv[:t_ctx, wa:2 * wa].reshape(batch, seq, heads_a, 2, HEAD))
            states["a_v"].append(qkv[:t_ctx, 2 * wa:].reshape(batch, seq, heads_a, LANE))
            lamp = jnp.stack([lam_q1_a[i], lam_k1_a[i], lam_q2_a[i], lam_k2_a[i]])
            o_p = _diff_attention(qkv, lamp, subln_a[i], None, None, row0=0, n_batch=batch,
                                  n_own=seq, heads_total=heads_a, lambda_init=lambda_init)
            o_s = _diff_attention(qkv, lamp, subln_a[i],
                                  cache_a_k[:, i].reshape(dec_batch * past, wa),
                                  cache_a_v[:, i].reshape(dec_batch * past, wa),
                                  row0=t_ctx, n_batch=dec_batch, n_own=n_lat, heads_total=heads_a,
                                  lambda_init=lambda_init)
            w_o = w_o_a[i]
        else:
            wq, wk, wv = jnp.split(w_qkv_b[i], [qw_b, qw_b + kvw_b], axis=1)
            dup = lambda w: jnp.concatenate([w.reshape(d, kv_heads_b, 1, HEAD)] * 2, axis=2).reshape(
                d, 2 * kvw_b)
            w_b = jnp.concatenate([wq, dup(wk), dup(wv)], axis=1).astype(BF16)
            gain = jnp.concatenate([
                _head_table(q_norm_b[i], q_heads_b, HEAD ** -0.5),
                _head_table(k_norm_b[i], 2 * kv_heads_b),
                jnp.ones((2 * kvw_b,), F32)]).reshape(1, qw_b + 4 * kvw_b)
            qkv = _qkv_project(x, mod, layer, g_mix[layer], w_b, gain, cos_tab, sin_tab, grp,
                               tm=tm_qkv, n_norm=qw_b + 2 * kvw_b, t_ctx=t_ctx, n_lat=n_lat)
            k_state = qkv[:t_ctx, qw_b:qw_b + 2 * kvw_b].reshape(batch, seq, kv_heads_b, 2, HEAD)
            v_state = qkv[:t_ctx, qw_b + 2 * kvw_b:].reshape(batch, seq, kv_heads_b, 2, HEAD)
            states["b_k"].append(k_state[:, :, :, 0])
            states["b_v"].append(v_state[:, :, :, 0])
            dup_c = lambda a: jnp.concatenate([a, a], axis=-1).reshape(dec_batch * past,
                                                                       2 * kvw_b)
            o_p = _sink_attention(qkv, sinks_b[i], None, None, row0=0, n_batch=batch, n_own=seq,
                                  q_width=qw_b, kv_heads=kv_heads_b, windowed=False)
            o_s = _sink_attention(qkv, sinks_b[i], dup_c(cache_b_k[:, i]), dup_c(cache_b_v[:, i]),
                                  row0=t_ctx, n_batch=dec_batch, n_own=n_lat, q_width=qw_b,
                                  kv_heads=kv_heads_b, windowed=True)
            w_o = w_o_b[i]

        x1, hp, route, gates, counts = _out_project(
            o_p, o_s, w_o.astype(BF16), x, mod, layer, g_ffn[layer], w_router[layer],
            b_router[layer], t_ctx=t_ctx, n_lat=n_lat)
        x = _moe_ffn(hp, route, gates, counts, x1, mod, layer, w_in, b_in, w_out, b_out,
                     t_ctx=t_ctx, n_lat=n_lat)

    return (x[:t_ctx].reshape(batch, seq, d), x[t_ctx:].reshape(dec_batch, n_lat, d),
            jnp.stack(states["a_k"], axis=1), jnp.stack(states["a_v"], axis=1),
            jnp.stack(states["b_k"], axis=1), jnp.stack(states["b_v"], axis=1))
```

```python
import functools
import math

import jax
import jax.numpy as jnp
from jax import lax
from jax.experimental import pallas as pl
from jax.experimental.pallas import tpu as pltpu

F32 = jnp.float32
BF16 = jnp.bfloat16
I32 = jnp.int32
U32 = jnp.uint32

LANE = 128
HEAD = 64
NORM_EPS = 1e-6
ROPE_BASE = 10000.0
GRID_W = 64
WINDOW = 128
TOP_K = 4
SWIGLU_LIMIT = 7.0
SWIGLU_ALPHA = 1.702
NEG_INF = -1e30
LOG2E = 1.4426950408889634
N_COND = 8
VMEM_LIMIT = 56 * 1024 * 1024
GATHER_PRIORITY = 1


def _params(n_axes):
    return pltpu.CompilerParams(
        dimension_semantics=("arbitrary",) * n_axes, vmem_limit_bytes=VMEM_LIMIT)


def _pick(n, prefs):
    for p in prefs:
        if n % p == 0:
            return p
    raise ValueError(f"no tile in {prefs} divides {n}")


def _dot(a, b):
    return jnp.dot(a, b, preferred_element_type=F32)


def _dot_nt(a, b):
    return lax.dot_general(a, b, (((1,), (1,)), ((), ())), preferred_element_type=F32)


def _split(v):
    hi = v.astype(BF16)
    lo = (v - hi.astype(F32)).astype(BF16)
    return hi, lo


def _dot3(a, b):
    a_hi, a_lo = _split(a)
    b_hi, b_lo = _split(b)
    return _dot(a_hi, b_hi) + _dot(a_lo, b_hi) + _dot(a_hi, b_lo)


def _ada_kernel(c_ref, w_ref, b_ref, o_ref):
    c = c_ref[...]
    s = c * jax.nn.sigmoid(c)
    o_ref[...] = _dot3(s, w_ref[...]) + b_ref[...]


def _ada_params(cond, w_ada, b_ada):
    depth, d, n = w_ada.shape
    tn = _pick(n, (512, 256, 128))
    return pl.pallas_call(
        _ada_kernel,
        out_shape=jax.ShapeDtypeStruct((depth, N_COND, n), F32),
        grid=(depth, n // tn),
        in_specs=[
            pl.BlockSpec((N_COND, d), lambda l, j: (0, 0)),
            pl.BlockSpec((None, d, tn), lambda l, j: (l, 0, j)),
            pl.BlockSpec((None, 1, tn), lambda l, j: (l, 0, j)),
        ],
        out_specs=pl.BlockSpec((None, N_COND, tn), lambda l, j: (l, 0, j)),
        compiler_params=_params(2),
        name="ada_params",
    )(cond, w_ada, b_ada.reshape(depth, 1, n))


def _modulate(x, g, sh, sc):
    xn = x * lax.rsqrt(jnp.mean(x * x, axis=-1, keepdims=True) + NORM_EPS)
    return xn * g * (1.0 + sc) + sh


def _qkv_kernel(x_ref, sh_ref, sc_ref, g_ref, w_ref, gain_ref, cos_ref, sin_ref, grp_ref,
                o_ref, h_scr, *, n_norm_tiles, tn):
    j = pl.program_id(1)

    @pl.when(j == 0)
    def _():
        h_scr[...] = _modulate(x_ref[...], g_ref[...], sh_ref[...], sc_ref[...]).astype(BF16)

    y = _dot(h_scr[...], w_ref[...])

    @pl.when(j < n_norm_tiles)
    def _():
        lane = lax.broadcasted_iota(I32, (1, LANE), 1)
        first = (lane & 16) == 0
        cos = cos_ref[...]
        sin = sin_ref[...]
        for s in range(tn // LANE):
            cols = slice(s * LANE, (s + 1) * LANE)
            ys = y[:, cols]
            ms = _dot((ys * ys).astype(BF16), grp_ref[...])
            yn = ys * lax.rsqrt(ms + NORM_EPS) * gain_ref[:, cols]
            rot = jnp.where(first, pltpu.roll(yn, LANE - 16, 1), pltpu.roll(yn, 16, 1))
            o_ref[:, cols] = yn * cos + rot * sin

    @pl.when(j >= n_norm_tiles)
    def _():
        o_ref[...] = y


def _cond_of_tile(i, n_ctx_tiles, tiles_per_seq):
    return jnp.where(i < n_ctx_tiles, 0, 1 + (i - n_ctx_tiles) // tiles_per_seq)


def _mod_spec(layer, which, d, cond_fn):
    return pl.BlockSpec((None, None, None, 1, d),
                        lambda i, *_: (layer, cond_fn(i), which, 0, 0))


def _qkv_project(x, mod, layer, g_mix, w, gain, cos_tab, sin_tab, grp, *, tm, n_norm, t_ctx, n_lat):
    t, d = x.shape
    n = w.shape[1]
    tn = _pick(math.gcd(n_norm, n - n_norm), (512, 256, 128))
    n_ctx_tiles = t_ctx // tm
    tiles_per_seq = n_lat // tm
    cond_fn = functools.partial(_cond_of_tile, n_ctx_tiles=n_ctx_tiles, tiles_per_seq=tiles_per_seq)

    def rope_idx(i, j):
        return (jnp.where(i < n_ctx_tiles, 0, 1 + (i - n_ctx_tiles) % tiles_per_seq), 0)

    return pl.pallas_call(
        functools.partial(_qkv_kernel, n_norm_tiles=n_norm // tn, tn=tn),
        out_shape=jax.ShapeDtypeStruct((t, n), F32),
        grid=(t // tm, n // tn),
        in_specs=[
            pl.BlockSpec((tm, d), lambda i, j: (i, 0)),
            _mod_spec(layer, 0, d, cond_fn),
            _mod_spec(layer, 1, d, cond_fn),
            pl.BlockSpec((1, d), lambda i, j: (0, 0)),
            pl.BlockSpec((d, tn), lambda i, j: (0, j)),
            pl.BlockSpec((1, tn), lambda i, j: (0, j)),
            pl.BlockSpec((tm, LANE), rope_idx),
            pl.BlockSpec((tm, LANE), rope_idx),
            pl.BlockSpec((LANE, LANE), lambda i, j: (0, 0)),
        ],
        out_specs=pl.BlockSpec((tm, tn), lambda i, j: (i, j)),
        scratch_shapes=[pltpu.VMEM((tm, d), BF16)],
        compiler_params=_params(2),
        name=f"qkv_project_{layer}",
    )(x, mod, mod, g_mix.reshape(1, d), w, gain, cos_tab, sin_tab, grp)


def _rope_tables(n_lat, tm):
    rows = n_lat // GRID_W
    row = jnp.broadcast_to(jnp.arange(rows, dtype=F32)[:, None], (rows, GRID_W)).reshape(-1)
    col = jnp.broadcast_to(jnp.arange(GRID_W, dtype=F32)[None, :], (rows, GRID_W)).reshape(-1)
    quarter = HEAD // 4
    inv = ROPE_BASE ** (-jnp.arange(quarter, dtype=F32) / quarter)
    ang_r = row[:, None] * inv[None, :]
    ang_c = col[:, None] * inv[None, :]
    ang = jnp.concatenate([ang_r, ang_r, ang_c, ang_c], axis=-1)
    sign = jnp.where((jnp.arange(HEAD) % 32) < 16, -1.0, 1.0).astype(F32)
    cos = jnp.cos(ang)
    sin = jnp.sin(ang) * sign
    cos = jnp.concatenate([cos, cos], axis=-1)
    sin = jnp.concatenate([sin, sin], axis=-1)
    cos = jnp.concatenate([jnp.ones((tm, LANE), F32), cos], axis=0)
    sin = jnp.concatenate([jnp.zeros((tm, LANE), F32), sin], axis=0)
    return cos, sin


def _attn_a_kernel(lamp_ref, sub_ref, q_ref, *rest, n_ctx, n_own, tq, tk, heads, lambda_init):
    if n_ctx:
        ck_ref, cv_ref, k_ref, v_ref, o_ref, kb, vb, acc = rest
    else:
        k_ref, v_ref, o_ref, kb, vb, acc = rest
    qi = pl.program_id(2)
    n_keys = n_own + n_ctx

    @pl.when(qi == 0)
    def _():
        kb[0:n_own, :] = k_ref[...].astype(BF16)
        vb[0:n_own, :] = v_ref[...].astype(BF16)
        if n_ctx:
            kb[n_own:n_keys, :] = ck_ref[...].astype(BF16)
            vb[n_own:n_keys, :] = cv_ref[...].astype(BF16)

    lp = lamp_ref[...]
    lam = (jnp.exp(jnp.sum(lp[0:1] * lp[1:2], axis=-1, keepdims=True))
           - jnp.exp(jnp.sum(lp[2:3] * lp[3:4], axis=-1, keepdims=True)) + lambda_init)
    lo = lax.broadcasted_iota(I32, (1, LANE), 1) < HEAD
    n_main = n_own // tk
    tail = n_keys - n_main * tk

    cols = [slice(h * LANE, (h + 1) * LANE) for h in range(heads)]
    q2 = []
    for h in range(heads):
        q = q_ref[:, cols[h]] * LOG2E
        q2.append(jnp.concatenate([jnp.where(lo, q, 0.0), jnp.where(lo, 0.0, q)],
                                  axis=0).astype(BF16))
    acc[...] = jnp.zeros_like(acc)

    def step(rows, carry):
        out = []
        for h in range(heads):
            m, l = carry[h]
            s = _dot_nt(q2[h], kb[rows, cols[h]])
            m_new = jnp.maximum(m, jnp.max(s, axis=-1, keepdims=True))
            alpha = jnp.exp2(m - m_new)
            p = jnp.exp2(s - m_new)
            acc[h] = alpha * acc[h] + _dot(p.astype(BF16), vb[rows, cols[h]])
            out.append((m_new, alpha * l + jnp.sum(p, axis=-1, keepdims=True)))
        return tuple(out)

    def body(j, carry):
        return step(pl.ds(pl.multiple_of(j * tk, tk), tk), carry)

    carry = tuple((jnp.full((2 * tq, 1), NEG_INF, F32), jnp.zeros((2 * tq, 1), F32))
                  for _ in range(heads))
    if n_main:
        carry = lax.fori_loop(0, n_main, body, carry)
    if tail:
        carry = step(slice(n_main * tk, n_keys), carry)
    for h in range(heads):
        on = acc[h] / carry[h][1]
        o = on[0:tq] - lam * on[tq:2 * tq]
        o = o * lax.rsqrt(jnp.mean(o * o, axis=-1, keepdims=True) + NORM_EPS)
        o_ref[:, cols[h]] = (o * sub_ref[...] * (1.0 - lambda_init)).astype(BF16)


def _diff_attention(qkv, lamp, sub_gain, cache_k, cache_v, *, row0, n_batch, n_own, heads_total,
                    lambda_init):
    width = heads_total * LANE
    heads = 2 if heads_total % 2 == 0 else 1
    gw = heads * LANE
    n_groups = heads_total // heads
    tq = _pick(n_own, (256, 128))
    tk = _pick(n_own, (2048, 1024, 512, 256, 128))
    n_ctx = 0 if cache_k is None else cache_k.shape[0] // n_batch
    q_tiles = n_own // tq
    q_row0 = row0 // tq
    kv_row0 = row0 // n_own
    sec = width // gw

    in_specs = [
        pl.BlockSpec((4, HEAD), lambda b, g, i: (0, 0)),
        pl.BlockSpec((1, LANE), lambda b, g, i: (0, 0)),
        pl.BlockSpec((tq, gw), lambda b, g, i: (q_row0 + b * q_tiles + i, g)),
    ]
    args = [lamp, sub_gain.reshape(1, LANE), qkv]
    if n_ctx:
        in_specs += [pl.BlockSpec((n_ctx, gw), lambda b, g, i: (b, g)),
                     pl.BlockSpec((n_ctx, gw), lambda b, g, i: (b, g))]
        args += [cache_k, cache_v]
    in_specs += [pl.BlockSpec((n_own, gw), lambda b, g, i: (kv_row0 + b, sec + g)),
                 pl.BlockSpec((n_own, gw), lambda b, g, i: (kv_row0 + b, 2 * sec + g))]
    args += [qkv, qkv]
    n_keys = n_own + n_ctx
    return pl.pallas_call(
        functools.partial(_attn_a_kernel, n_ctx=n_ctx, n_own=n_own, tq=tq, tk=tk, heads=heads,
                          lambda_init=lambda_init),
        out_shape=jax.ShapeDtypeStruct((n_batch * n_own, width), BF16),
        grid=(n_batch, n_groups, q_tiles),
        in_specs=in_specs,
        out_specs=pl.BlockSpec((tq, gw), lambda b, g, i: (b * q_tiles + i, g)),
        scratch_shapes=[pltpu.VMEM((n_keys, gw), BF16), pltpu.VMEM((n_keys, gw), BF16),
                        pltpu.VMEM((heads, 2 * tq, LANE), F32)],
        compiler_params=_params(3),
        name="diff_attention_ctx" if n_ctx else "diff_attention_self",
    )(*args)


def _attn_b_kernel(sink_ref, q_ref, *rest, n_ctx, n_own, tq, windowed, slabs):
    if n_ctx:
        ck_ref, cv_ref, k_ref, v_ref, o_ref, kb, vb = rest
    else:
        k_ref, v_ref, o_ref, kb, vb = rest
    g = pl.program_id(1)
    qi = pl.program_id(2)
    pad = WINDOW if windowed else 0
    own0 = n_ctx + pad

    @pl.when(qi == 0)
    def _():
        if n_ctx:
            kb[0:n_ctx, :] = ck_ref[...].astype(BF16)
            vb[0:n_ctx, :] = cv_ref[...].astype(BF16)
        if pad:
            zeros = jnp.zeros((pad, LANE), BF16)
            for ref in (kb, vb):
                ref[n_ctx:own0, :] = zeros
                ref[own0 + n_own:own0 + n_own + pad, :] = zeros
        kb[own0:own0 + n_own, :] = k_ref[...].astype(BF16)
        vb[own0:own0 + n_own, :] = v_ref[...].astype(BF16)

    if windowed:
        wlen = tq + 2 * pad
        st = pl.multiple_of(n_ctx + qi * tq, tq)
        k_w = kb[pl.ds(st, wlen), :]
        v_w = vb[pl.ds(st, wlen), :]
        r = lax.broadcasted_iota(I32, (2 * tq, wlen), 0) & (tq - 1)
        c = lax.broadcasted_iota(I32, (2 * tq, wlen), 1)
        kpos = qi * tq - pad + c
        d = c - r
        mask = (d >= 0) & (d <= 2 * pad) & (kpos >= 0) & (kpos < n_own)
    else:
        k_w = kb[own0:own0 + n_own, :]
        v_w = vb[own0:own0 + n_own, :]
        mask = None
    if n_ctx:
        k_c = kb[0:n_ctx, :]
        v_c = vb[0:n_ctx, :]

    lo = lax.broadcasted_iota(I32, (1, LANE), 1) < HEAD
    top = lax.broadcasted_iota(I32, (2 * tq, 1), 0) < tq
    for t in range(slabs):
        cols = slice(t * LANE, (t + 1) * LANE)
        q = q_ref[:, cols] * LOG2E
        q2 = jnp.concatenate([jnp.where(lo, q, 0.0), jnp.where(lo, 0.0, q)], axis=0).astype(BF16)
        head0 = g * (2 * slabs) + 2 * t
        sk = jnp.where(top, sink_ref[head0], sink_ref[head0 + 1]) * LOG2E
        s_w = _dot_nt(q2, k_w)
        if mask is not None:
            s_w = jnp.where(mask, s_w, NEG_INF)
        m = jnp.maximum(jnp.max(s_w, axis=-1, keepdims=True), sk)
        if n_ctx:
            s_c = _dot_nt(q2, k_c)
            m = jnp.maximum(m, jnp.max(s_c, axis=-1, keepdims=True))
        p_w = jnp.exp2(s_w - m)
        den = jnp.sum(p_w, axis=-1, keepdims=True) + jnp.exp2(sk - m)
        o = _dot(p_w.astype(BF16), v_w)
        if n_ctx:
            p_c = jnp.exp2(s_c - m)
            den = den + jnp.sum(p_c, axis=-1, keepdims=True)
            o = o + _dot(p_c.astype(BF16), v_c)
        o = o / den
        o_ref[:, cols] = jnp.where(lo, o[0:tq], o[tq:2 * tq]).astype(BF16)


def _sink_attention(qkv, sinks, cache_k, cache_v, *, row0, n_batch, n_own, q_width, kv_heads,
                    windowed):
    gw = q_width // kv_heads
    slabs = gw // LANE
    tq = _pick(n_own, (256, 128))
    n_ctx = 0 if cache_k is None else cache_k.shape[0] // n_batch
    q_tiles = n_own // tq
    q_row0 = row0 // tq
    kv_row0 = row0 // n_own
    k_col0 = q_width // LANE
    v_col0 = k_col0 + kv_heads
    pad = WINDOW if windowed else 0

    in_specs = [
        pl.BlockSpec(memory_space=pltpu.SMEM),
        pl.BlockSpec((tq, gw), lambda b, g, i: (q_row0 + b * q_tiles + i, g)),
    ]
    args = [sinks, qkv]
    if n_ctx:
        in_specs += [pl.BlockSpec((n_ctx, LANE), lambda b, g, i: (b, g)),
                     pl.BlockSpec((n_ctx, LANE), lambda b, g, i: (b, g))]
        args += [cache_k, cache_v]
    in_specs += [pl.BlockSpec((n_own, LANE), lambda b, g, i: (kv_row0 + b, k_col0 + g)),
                 pl.BlockSpec((n_own, LANE), lambda b, g, i: (kv_row0 + b, v_col0 + g))]
    args += [qkv, qkv]
    n_rows = n_ctx + n_own + 2 * pad
    return pl.pallas_call(
        functools.partial(_attn_b_kernel, n_ctx=n_ctx, n_own=n_own, tq=tq, windowed=windowed,
                          slabs=slabs),
        out_shape=jax.ShapeDtypeStruct((n_batch * n_own, q_width), BF16),
        grid=(n_batch, kv_heads, q_tiles),
        in_specs=in_specs,
        out_specs=pl.BlockSpec((tq, gw), lambda b, g, i: (b * q_tiles + i, g)),
        scratch_shapes=[pltpu.VMEM((n_rows, LANE), BF16), pltpu.VMEM((n_rows, LANE), BF16)],
        compiler_params=_params(3),
        name="sink_attention_window" if windowed else "sink_attention_self",
    )(*args)


def _out_kernel(oc_ref, ol_ref, wo_ref, x_ref, gt_ref, g_ref, sh_ref, sc_ref, wr_ref, br_ref,
                x1_ref, hp_ref, route_ref, gate_ref, cnt_ref, carry, *, n_exp, tm, n_ctx_tiles):
    i = pl.program_id(0)

    @pl.when(i == 0)
    def _():
        carry[...] = jnp.zeros_like(carry)

    o = jnp.where(i < n_ctx_tiles, oc_ref[...], ol_ref[...])
    x1 = x_ref[...] + gt_ref[...] * _dot(o, wo_ref[...])
    x1_ref[...] = x1
    h = _modulate(x1, g_ref[...], sh_ref[...], sc_ref[...])
    h_hi, h_lo = _split(h)
    half = h.shape[1] // 2
    ua = pltpu.bitcast(h_hi[:, :half].astype(F32), U32)
    ub = pltpu.bitcast(h_hi[:, half:].astype(F32), U32)
    hp_ref[...] = ua | (ub >> 16)

    w_hi, w_lo = _split(wr_ref[...])
    logits = _dot(h_hi, w_hi) + _dot(h_lo, w_hi) + _dot(h_hi, w_lo) + br_ref[...]

    eio = lax.broadcasted_iota(I32, (tm, n_exp), 1).astype(F32)
    work = logits
    tops, idxs, hots = [], [], []
    for _ in range(TOP_K):
        mk = jnp.max(work, axis=-1, keepdims=True)
        ik = jnp.min(jnp.where(work == mk, eio, float(n_exp)), axis=-1, keepdims=True)
        hot = eio == ik
        tops.append(mk)
        idxs.append(ik)
        hots.append(hot)
        work = jnp.where(hot, -3e38, work)
    es = [jnp.exp(t - tops[0]) for t in tops]
    den = es[0] + es[1] + es[2] + es[3]

    hot_all = jnp.zeros((tm, n_exp), F32)
    for hot in hots:
        hot_all = hot_all + jnp.where(hot, 1.0, 0.0)
    tri = jnp.where(lax.broadcasted_iota(I32, (tm, tm), 0) > lax.broadcasted_iota(I32, (tm, tm), 1),
                    1.0, 0.0).astype(BF16)
    base = _dot(tri, hot_all.astype(BF16)) + carry[...]
    ranks = [jnp.sum(jnp.where(hot, base, 0.0), axis=-1, keepdims=True).astype(I32) for hot in hots]
    carry[...] = carry[...] + jnp.sum(hot_all, axis=0, keepdims=True)
    cnt_ref[...] = carry[...]

    col = lax.broadcasted_iota(I32, (tm, LANE), 1)
    route = jnp.zeros((tm, LANE), I32)
    gate = jnp.zeros((tm, LANE), F32)
    for k in range(TOP_K):
        route = jnp.where(col == k, idxs[k].astype(I32), route)
        route = jnp.where(col == TOP_K + k, ranks[k], route)
        gate = jnp.where(col == k, es[k] / den, gate)
    route_ref[...] = route
    gate_ref[...] = gate


def _out_project(o_ctx, o_lat, w_o, x, mod, layer, g_ffn, w_router, b_router, *, t_ctx, n_lat):
    t, d = x.shape
    kdim = o_ctx.shape[1]
    n_exp = w_router.shape[1]
    tm = _pick(math.gcd(t_ctx, n_lat), (512, 256, 128))
    n_ctx_tiles = t_ctx // tm
    cond_fn = functools.partial(_cond_of_tile, n_ctx_tiles=n_ctx_tiles, tiles_per_seq=n_lat // tm)
    row = lambda i: (i, 0)
    fixed = lambda i: (0, 0)
    return pl.pallas_call(
        functools.partial(_out_kernel, n_exp=n_exp, tm=tm, n_ctx_tiles=n_ctx_tiles),
        out_shape=(jax.ShapeDtypeStruct((t, d), F32), jax.ShapeDtypeStruct((t, d // 2), U32),
                   jax.ShapeDtypeStruct((t, LANE), I32), jax.ShapeDtypeStruct((t, LANE), F32),
                   jax.ShapeDtypeStruct((1, n_exp), F32)),
        grid=(t // tm,),
        in_specs=[
            pl.BlockSpec((tm, kdim), lambda i: (jnp.minimum(i, n_ctx_tiles - 1), 0)),
            pl.BlockSpec((tm, kdim), lambda i: (jnp.maximum(i - n_ctx_tiles, 0), 0)),
            pl.BlockSpec((kdim, d), fixed),
            pl.BlockSpec((tm, d), row),
            _mod_spec(layer, 2, d, cond_fn),
            pl.BlockSpec((1, d), fixed),
            _mod_spec(layer, 3, d, cond_fn),
            _mod_spec(layer, 4, d, cond_fn),
            pl.BlockSpec((d, n_exp), fixed),
            pl.BlockSpec((1, n_exp), fixed),
        ],
        out_specs=(pl.BlockSpec((tm, d), row), pl.BlockSpec((tm, d // 2), row),
                   pl.BlockSpec((tm, LANE), row), pl.BlockSpec((tm, LANE), row),
                   pl.BlockSpec((1, n_exp), fixed)),
        scratch_shapes=[pltpu.VMEM((1, n_exp), F32)],
        compiler_params=_params(1),
        name=f"out_project_route_{layer}",
    )(o_ctx, o_lat, w_o, x, mod, g_ffn.reshape(1, d), mod, mod, w_router,
      b_router.reshape(1, n_exp))


def _moe_kernel(te_ref, nv_ref, tok_hbm, hp_hbm, wg_ref, wu_ref, bg_ref, bu_ref, wo_ref, bo_ref,
                y_ref, idx, xs, xb, act, isem, gsem, *, tm, nf1, rows_per_step):
    i = pl.program_id(0)
    s = pl.program_id(1)
    nv = nv_ref[0]
    cur = i % 2
    nxt = 1 - cur

    def fetch_idx(tile, buf):
        cp = pltpu.make_async_copy(tok_hbm.at[tile], idx.at[buf], isem)
        cp.start()
        cp.wait()

    row_words = xs.shape[2]

    def row_copy(buf, r):
        start = pl.multiple_of(idx[buf, 0, r], row_words)
        return pltpu.make_async_copy(hp_hbm.at[pl.ds(start, row_words)], xs.at[buf, r], gsem.at[buf])

    @pl.when(jnp.logical_and(i == 0, s == 0))
    def _():
        fetch_idx(0, 0)

        def issue(r, c):
            row_copy(0, r).start(priority=GATHER_PRIORITY)
            return c

        lax.fori_loop(0, tm, issue, 0)

    @pl.when(jnp.logical_and(i < nv, s == 0))
    def _():
        pltpu.make_async_copy(xs.at[nxt], xs.at[cur], gsem.at[cur]).wait()
        u = xs[cur]
        xb[:, :row_words] = pltpu.bitcast(u & jnp.uint32(0xFFFF0000), F32).astype(BF16)
        xb[:, row_words:] = pltpu.bitcast(u << 16, F32).astype(BF16)

    @pl.when(jnp.logical_and(i + 1 < nv, s == 0))
    def _():
        fetch_idx(i + 1, nxt)

    def activation_chunk():
        wg = wg_ref[...].astype(BF16)
        wu = wu_ref[...].astype(BF16)
        for r in range(2):
            rows = slice(r * (tm // 2), (r + 1) * (tm // 2))
            x = xb[rows, :]
            g = _dot(x, wg) + bg_ref[...]
            u = _dot(x, wu) + bu_ref[...]
            g = jnp.minimum(g, SWIGLU_LIMIT)
            u = jnp.clip(u, -SWIGLU_LIMIT, SWIGLU_LIMIT)
            act[s, rows, :] = (g * jax.nn.sigmoid(SWIGLU_ALPHA * g) * (u + 1.0)).astype(BF16)

    @pl.when(jnp.logical_and(i + 1 < nv, s < nf1))
    def _():
        r0 = s * rows_per_step
        for r in range(rows_per_step):
            row_copy(nxt, r0 + r).start(priority=GATHER_PRIORITY)
        activation_chunk()

    @pl.when(jnp.logical_and(i + 1 == nv, s < nf1))
    def _():
        activation_chunk()

    @pl.when(jnp.logical_and(i < nv, s >= nf1))
    def _():
        a = jnp.concatenate([act[c] for c in range(nf1)], axis=1)
        y_ref[...] = _dot(a, wo_ref[...].astype(BF16)) + bo_ref[...]

    @pl.when(jnp.logical_and(i >= nv, s >= nf1))
    def _():
        y_ref[...] = jnp.zeros_like(y_ref)


def _moe_experts(hp, slot_token, tile_expert, n_valid, layer, w_in, b_in, w_out, b_out, *, tm):
    cap = slot_token.shape[0]
    depth, n_exp, d, two_ff = w_in.shape
    d_ff = two_ff // 2
    tf = _pick(d_ff, (256, 128))
    nf1 = d_ff // tf
    tn = _pick(d, (512, 256, 128))
    nf2 = d // tn
    row_words = hp.shape[1]
    rows_per_step = tm // nf1

    def in_chunk(i, s, nv):
        return jnp.where(i < nv[0], jnp.minimum(s, nf1 - 1), nf1 - 1)

    def out_chunk(i, s, nv):
        return jnp.where(i < nv[0], jnp.maximum(s - nf1, 0), nf2 - 1)

    def wo_idx(i, s, te, nv):
        hold = jnp.logical_and(s == 0, i > 0)
        e = jnp.where(hold, te[jnp.maximum(i - 1, 0)], te[i])
        return (layer, e, 0, jnp.where(hold, nf2 - 1, out_chunk(i, s, nv)))

    wblk = (None, None, d, tf)
    bblk = (None, None, 1, tf)
    grid_spec = pltpu.PrefetchScalarGridSpec(
        num_scalar_prefetch=2,
        grid=(cap // tm, nf1 + nf2),
        in_specs=[
            pl.BlockSpec(memory_space=pl.ANY),
            pl.BlockSpec(memory_space=pl.ANY),
            pl.BlockSpec(wblk, lambda i, s, te, nv: (layer, te[i], 0, in_chunk(i, s, nv))),
            pl.BlockSpec(wblk, lambda i, s, te, nv: (layer, te[i], 0, nf1 + in_chunk(i, s, nv))),
            pl.BlockSpec(bblk, lambda i, s, te, nv: (layer, te[i], 0, in_chunk(i, s, nv))),
            pl.BlockSpec(bblk, lambda i, s, te, nv: (layer, te[i], 0, nf1 + in_chunk(i, s, nv))),
            pl.BlockSpec((None, None, d_ff, tn), wo_idx),
            pl.BlockSpec((None, None, 1, tn),
                         lambda i, s, te, nv: (layer, te[i], 0, out_chunk(i, s, nv))),
        ],
        out_specs=pl.BlockSpec((tm, tn), lambda i, s, te, nv: (i, jnp.maximum(s - nf1, 0))),
        scratch_shapes=[pltpu.SMEM((2, 1, tm), I32), pltpu.VMEM((2, tm, row_words), U32),
                        pltpu.VMEM((tm, d), BF16), pltpu.VMEM((nf1, tm, tf), BF16),
                        pltpu.SemaphoreType.DMA, pltpu.SemaphoreType.DMA((2,))],
    )
    b_in4 = b_in.reshape(depth, n_exp, 1, two_ff)
    return pl.pallas_call(
        functools.partial(_moe_kernel, tm=tm, nf1=nf1, rows_per_step=rows_per_step),
        out_shape=jax.ShapeDtypeStruct((cap, d), F32),
        grid_spec=grid_spec,
        compiler_params=_params(2),
        name=f"moe_experts_{layer}",
    )(tile_expert, n_valid, (slot_token * row_words).reshape(cap // tm, 1, tm), hp.reshape(-1),
      w_in, w_in, b_in4, b_in4, w_out, b_out.reshape(depth, n_exp, 1, d))


def _combine_kernel(slot_hbm, gate_ref, x_ref, gt_ref, y_hbm, o_ref, idx, buf, isem, sem, *, tc):
    i = pl.program_id(0)
    n = pl.num_programs(0)
    cur = i % 2
    nxt = 1 - cur

    def row_copy(b, r, k):
        return pltpu.make_async_copy(y_hbm.at[pl.ds(idx[b, 0, r * TOP_K + k], 1)],
                                     buf.at[b, k, pl.ds(r, 1)], sem.at[b])

    def start_tile(tile, b):
        cp = pltpu.make_async_copy(slot_hbm.at[tile], idx.at[b], isem)
        cp.start()
        cp.wait()

        def issue(r, c):
            for k in range(TOP_K):
                row_copy(b, r, k).start(priority=k % 2)
            return c

        lax.fori_loop(0, tc, issue, 0)

    @pl.when(i == 0)
    def _():
        start_tile(0, 0)

    @pl.when(i + 1 < n)
    def _():
        start_tile(i + 1, nxt)

    pltpu.make_async_copy(buf.at[nxt], buf.at[cur], sem.at[cur]).wait()
    gate = gate_ref[...]
    f = gate[:, 0:1] * buf[cur, 0]
    for k in range(1, TOP_K):
        f = f + gate[:, k:k + 1] * buf[cur, k]
    o_ref[...] = x_ref[...] + gt_ref[...] * f


def _moe_combine(slot, gates, x1, mod, layer, y_sorted, *, t_ctx, n_lat):
    t, d = x1.shape
    tc = _pick(math.gcd(t_ctx, n_lat), (128,))
    cond_fn = functools.partial(_cond_of_tile, n_ctx_tiles=t_ctx // tc, tiles_per_seq=n_lat // tc)
    row = lambda i: (i, 0)
    return pl.pallas_call(
        functools.partial(_combine_kernel, tc=tc),
        out_shape=jax.ShapeDtypeStruct((t, d), F32),
        grid=(t // tc,),
        in_specs=[
            pl.BlockSpec(memory_space=pl.ANY),
            pl.BlockSpec((tc, LANE), row),
            pl.BlockSpec((tc, d), row),
            _mod_spec(layer, 5, d, cond_fn),
            pl.BlockSpec(memory_space=pl.ANY),
        ],
        out_specs=pl.BlockSpec((tc, d), row),
        scratch_shapes=[pltpu.SMEM((2, 1, tc * TOP_K), I32), pltpu.VMEM((2, TOP_K, tc, d), F32),
                        pltpu.SemaphoreType.DMA, pltpu.SemaphoreType.DMA((2,))],
        compiler_params=_params(1),
        name="moe_combine",
    )(slot.reshape(t // tc, 1, tc * TOP_K), gates, x1, mod, y_sorted)


def _moe_tile(n_assign, n_exp):
    per_expert = n_assign // n_exp
    for tm in (1024, 512, 256):
        if per_expert >= 3 * tm:
            return tm
    return 128


def _moe_ffn(hp, route, gates, counts, x1, mod, layer, w_in, b_in, w_out, b_out, *, t_ctx, n_lat):
    t = x1.shape[0]
    n_exp = w_in.shape[1]
    n_assign = t * TOP_K
    tm = _moe_tile(n_assign, n_exp)
    cap = -(-(n_assign + n_exp * (tm - 1)) // tm) * tm
    n_tiles = cap // tm

    counts = counts.reshape(n_exp).astype(I32)
    padded = (counts + tm - 1) // tm * tm
    pend = jnp.cumsum(padded)
    pstart = pend - padded
    idx = route[:, :TOP_K]
    slot = pstart[idx] + route[:, TOP_K:2 * TOP_K]
    token_of = jnp.broadcast_to(jnp.arange(t, dtype=I32)[:, None], (t, TOP_K))
    slot_token = jnp.zeros((cap,), I32).at[slot.reshape(-1)].set(
        token_of.reshape(-1), unique_indices=True)
    n_valid = (pend[-1] // tm).astype(I32)
    tile_start = jnp.minimum(jnp.arange(n_tiles, dtype=I32), n_valid - 1) * tm
    tile_expert = jnp.minimum(jnp.searchsorted(pend, tile_start, side='right'),
                              n_exp - 1).astype(I32)

    y_sorted = _moe_experts(hp, slot_token, tile_expert, n_valid.reshape(1), layer, w_in, b_in,
                            w_out, b_out, tm=tm)
    return _moe_combine(slot, gates, x1, mod, layer, y_sorted, t_ctx=t_ctx, n_lat=n_lat)


def _head_table(gain, n_heads, scale=1.0):
    return jnp.tile(gain * scale, n_heads)


def kernel(x_prompt, x_sample, cache_a_k, cache_a_v, cache_b_k, cache_b_v, c, c_ctx, w_ada, b_ada, g_mix, g_ffn, w_qkv_a, q_norm_a, k_norm_a, lam_q1_a, lam_k1_a, lam_q2_a, lam_k2_a, subln_a, w_o_a, w_qkv_b, q_norm_b, k_norm_b, sinks_b, w_o_b, w_router, b_router, w_in, b_in, w_out, b_out):
    batch, seq, d = x_prompt.shape
    dec_batch, n_lat, _ = x_sample.shape
    depth = w_ada.shape[0]
    past = cache_a_k.shape[2]
    heads_a = cache_a_k.shape[3]
    kv_heads_b = cache_b_k.shape[3]
    q_heads_b = sinks_b.shape[1]
    t_ctx = batch * seq
    t_lat = dec_batch * n_lat
    assert dec_batch + 1 <= N_COND

    x = jnp.concatenate([x_prompt.reshape(t_ctx, d), x_sample.reshape(t_lat, d)], axis=0)
    cond = jnp.concatenate([c_ctx[None, :], c, jnp.zeros((N_COND - 1 - dec_batch, d), F32)], axis=0)
    mod = _ada_params(cond, w_ada, b_ada).reshape(depth, N_COND, 6, 1, d)

    tm_qkv = _pick(math.gcd(t_ctx, n_lat), (1024, 512, 256, 128))
    cos_tab, sin_tab = _rope_tables(n_lat, tm_qkv)
    grp = jnp.where((jnp.arange(LANE)[:, None] // HEAD) == (jnp.arange(LANE)[None, :] // HEAD),
                    1.0 / HEAD, 0.0).astype(BF16)

    wa = heads_a * LANE
    qw_b = q_heads_b * HEAD
    kvw_b = kv_heads_b * HEAD
    states = {"a_k": [], "a_v": [], "b_k": [], "b_v": []}

    for layer in range(depth):
        i = layer // 2
        if layer % 2 == 0:
            lambda_init = 0.8 - 0.6 * math.exp(-0.3 * layer)
            gain = jnp.concatenate([
                _head_table(q_norm_a[i], 2 * heads_a, HEAD ** -0.5),
                _head_table(k_norm_a[i], 2 * heads_a),
                jnp.ones((wa,), F32)]).reshape(1, 3 * wa)
            qkv = _qkv_project(x, mod, layer, g_mix[layer], w_qkv_a[i].astype(BF16), gain,
                               cos_tab, sin_tab, grp, tm=tm_qkv, n_norm=2 * wa, t_ctx=t_ctx, n_lat=n_lat)
            states["a_k"].append(qkv[:t_ctx, wa:2 * wa].reshape(batch, seq, heads_a, 2, HEAD))
            states["a_v"].append(qkv[:t_ctx, 2 * wa:].reshape(batch, seq, heads_a, LANE))
            lamp = jnp.stack([lam_q1_a[i], lam_k1_a[i], lam_q2_a[i], lam_k2_a[i]])
            o_p = _diff_attention(qkv, lamp, subln_a[i], None, None, row0=0, n_batch=batch,
                                  n_own=seq, heads_total=heads_a, lambda_init=lambda_init)
            o_s = _diff_attention(qkv, lamp, subln_a[i],
                                  cache_a_k[:, i].reshape(dec_batch * past, wa),
                                  cache_a_v[:, i].reshape(dec_batch * past, wa),
                                  row0=t_ctx, n_batch=dec_batch, n_own=n_lat, heads_total=heads_a,
                                  lambda_init=lambda_init)
            w_o = w_o_a[i]
        else:
            wq, wk, wv = jnp.split(w_qkv_b[i], [qw_b, qw_b + kvw_b], axis=1)
            dup = lambda w: jnp.concatenate([w.reshape(d, kv_heads_b, 1, HEAD)] * 2, axis=2).reshape(
                d, 2 * kvw_b)
            w_b = jnp.concatenate([wq, dup(wk), dup(wv)], axis=1).astype(BF16)
            gain = jnp.concatenate([
                _head_table(q_norm_b[i], q_heads_b, HEAD ** -0.5),
                _head_table(k_norm_b[i], 2 * kv_heads_b),
                jnp.ones((2 * kvw_b,), F32)]).reshape(1, qw_b + 4 * kvw_b)
            qkv = _qkv_project(x, mod, layer, g_mix[layer], w_b, gain, cos_tab, sin_tab, grp,
                               tm=tm_qkv, n_norm=qw_b + 2 * kvw_b, t_ctx=t_ctx, n_lat=n_lat)
            k_state = qkv[:t_ctx, qw_b:qw_b + 2 * kvw_b].reshape(batch, seq, kv_heads_b, 2, HEAD)
            v_state = qkv[:t_ctx, qw_b + 2 * kvw_b:].reshape(batch, seq, kv_heads_b, 2, HEAD)
            states["b_k"].append(k_state[:, :, :, 0])
            states["b_v"].append(v_state[:, :, :, 0])
            dup_c = lambda a: jnp.concatenate([a, a], axis=-1).reshape(dec_batch * past,
                                                                       2 * kvw_b)
            o_p = _sink_attention(qkv, sinks_b[i], None, None, row0=0, n_batch=batch, n_own=seq,
                                  q_width=qw_b, kv_heads=kv_heads_b, windowed=False)
            o_s = _sink_attention(qkv, sinks_b[i], dup_c(cache_b_k[:, i]), dup_c(cache_b_v[:, i]),
                                  row0=t_ctx, n_batch=dec_batch, n_own=n_lat, q_width=qw_b,
                                  kv_heads=kv_heads_b, windowed=True)
            w_o = w_o_b[i]

        x1, hp, route, gates, counts = _out_project(
            o_p, o_s, w_o.astype(BF16), x, mod, layer, g_ffn[layer], w_router[layer],
            b_router[layer], t_ctx=t_ctx, n_lat=n_lat)
        x = _moe_ffn(hp, route, gates, counts, x1, mod, layer, w_in, b_in, w_out, b_out,
                     t_ctx=t_ctx, n_lat=n_lat)

    return (x[:t_ctx].reshape(batch, seq, d), x[t_ctx:].reshape(dec_batch, n_lat, d),
            jnp.stack(states["a_k"], axis=1), jnp.stack(states["a_v"], axis=1),
            jnp.stack(states["b_k"], axis=1), jnp.stack(states["b_v"], axis=1))
```

```python
import functools
import math

import jax
import jax.numpy as jnp
from jax import lax
from jax.experimental import pallas as pl
from jax.experimental.pallas import tpu as pltpu

F32 = jnp.float32
BF16 = jnp.bfloat16
I32 = jnp.int32
U32 = jnp.uint32

LANE = 128
HEAD = 64
NORM_EPS = 1e-6
ROPE_BASE = 10000.0
GRID_W = 64
WINDOW = 128
TOP_K = 4
SWIGLU_LIMIT = 7.0
SWIGLU_ALPHA = 1.702
NEG_INF = -1e30
LOG2E = 1.4426950408889634
N_COND = 8
VMEM_LIMIT = 56 * 1024 * 1024
GATHER_PRIORITY = 1


def _params(n_axes):
    return pltpu.CompilerParams(
        dimension_semantics=("arbitrary",) * n_axes, vmem_limit_bytes=VMEM_LIMIT)


def _pick(n, prefs):
    for p in prefs:
        if n % p == 0:
            return p
    raise ValueError(f"no tile in {prefs} divides {n}")


def _dot(a, b):
    return jnp.dot(a, b, preferred_element_type=F32)


def _dot_nt(a, b):
    return lax.dot_general(a, b, (((1,), (1,)), ((), ())), preferred_element_type=F32)


def _split(v):
    hi = v.astype(BF16)
    lo = (v - hi.astype(F32)).astype(BF16)
    return hi, lo


def _dot3(a, b):
    a_hi, a_lo = _split(a)
    b_hi, b_lo = _split(b)
    return _dot(a_hi, b_hi) + _dot(a_lo, b_hi) + _dot(a_hi, b_lo)


def _ada_kernel(c_ref, w_ref, b_ref, o_ref):
    c = c_ref[...]
    s = c * jax.nn.sigmoid(c)
    o_ref[...] = _dot3(s, w_ref[...]) + b_ref[...]


def _ada_params(cond, w_ada, b_ada):
    depth, d, n = w_ada.shape
    tn = _pick(n, (512, 256, 128))
    return pl.pallas_call(
        _ada_kernel,
        out_shape=jax.ShapeDtypeStruct((depth, N_COND, n), F32),
        grid=(depth, n // tn),
        in_specs=[
            pl.BlockSpec((N_COND, d), lambda l, j: (0, 0)),
            pl.BlockSpec((None, d, tn), lambda l, j: (l, 0, j)),
            pl.BlockSpec((None, 1, tn), lambda l, j: (l, 0, j)),
        ],
        out_specs=pl.BlockSpec((None, N_COND, tn), lambda l, j: (l, 0, j)),
        compiler_params=_params(2),
        name="ada_params",
    )(cond, w_ada, b_ada.reshape(depth, 1, n))


def _modulate(x, g, sh, sc):
    xn = x * lax.rsqrt(jnp.mean(x * x, axis=-1, keepdims=True) + NORM_EPS)
    return xn * g * (1.0 + sc) + sh


def _qkv_kernel(x_ref, sh_ref, sc_ref, g_ref, w_ref, gain_ref, cos_ref, sin_ref, grp_ref,
                o_ref, h_scr, *, n_norm_tiles, tn):
    j = pl.program_id(1)

    @pl.when(j == 0)
    def _():
        h_scr[...] = _modulate(x_ref[...], g_ref[...], sh_ref[...], sc_ref[...]).astype(BF16)

    y = _dot(h_scr[...], w_ref[...])

    @pl.when(j < n_norm_tiles)
    def _():
        lane = lax.broadcasted_iota(I32, (1, LANE), 1)
        first = (lane & 16) == 0
        cos = cos_ref[...]
        sin = sin_ref[...]
        for s in range(tn // LANE):
            cols = slice(s * LANE, (s + 1) * LANE)
            ys = y[:, cols]
            ms = _dot((ys * ys).astype(BF16), grp_ref[...])
            yn = ys * lax.rsqrt(ms + NORM_EPS) * gain_ref[:, cols]
            rot = jnp.where(first, pltpu.roll(yn, LANE - 16, 1), pltpu.roll(yn, 16, 1))
            o_ref[:, cols] = yn * cos + rot * sin

    @pl.when(j >= n_norm_tiles)
    def _():
        o_ref[...] = y


def _cond_of_tile(i, n_ctx_tiles, tiles_per_seq):
    return jnp.where(i < n_ctx_tiles, 0, 1 + (i - n_ctx_tiles) // tiles_per_seq)


def _mod_spec(layer, which, d, cond_fn):
    return pl.BlockSpec((None, None, None, 1, d),
                        lambda i, *_: (layer, cond_fn(i), which, 0, 0))


def _qkv_project(x, mod, layer, g_mix, w, gain, cos_tab, sin_tab, grp, *, tm, n_norm, t_ctx, n_lat):
    t, d = x.shape
    n = w.shape[1]
    tn = _pick(math.gcd(n_norm, n - n_norm), (512, 256, 128))
    n_ctx_tiles = t_ctx // tm
    tiles_per_seq = n_lat // tm
    cond_fn = functools.partial(_cond_of_tile, n_ctx_tiles=n_ctx_tiles, tiles_per_seq=tiles_per_seq)

    def rope_idx(i, j):
        return (jnp.where(i < n_ctx_tiles, 0, 1 + (i - n_ctx_tiles) % tiles_per_seq), 0)

    return pl.pallas_call(
        functools.partial(_qkv_kernel, n_norm_tiles=n_norm // tn, tn=tn),
        out_shape=jax.ShapeDtypeStruct((t, n), F32),
        grid=(t // tm, n // tn),
        in_specs=[
            pl.BlockSpec((tm, d), lambda i, j: (i, 0)),
            _mod_spec(layer, 0, d, cond_fn),
            _mod_spec(layer, 1, d, cond_fn),
            pl.BlockSpec((1, d), lambda i, j: (0, 0)),
            pl.BlockSpec((d, tn), lambda i, j: (0, j)),
            pl.BlockSpec((1, tn), lambda i, j: (0, j)),
            pl.BlockSpec((tm, LANE), rope_idx),
            pl.BlockSpec((tm, LANE), rope_idx),
            pl.BlockSpec((LANE, LANE), lambda i, j: (0, 0)),
        ],
        out_specs=pl.BlockSpec((tm, tn), lambda i, j: (i, j)),
        scratch_shapes=[pltpu.VMEM((tm, d), BF16)],
        compiler_params=_params(2),
        name=f"qkv_project_{layer}",
    )(x, mod, mod, g_mix.reshape(1, d), w, gain, cos_tab, sin_tab, grp)


def _rope_tables(n_lat, tm):
    rows = n_lat // GRID_W
    row = jnp.broadcast_to(jnp.arange(rows, dtype=F32)[:, None], (rows, GRID_W)).reshape(-1)
    col = jnp.broadcast_to(jnp.arange(GRID_W, dtype=F32)[None, :], (rows, GRID_W)).reshape(-1)
    quarter = HEAD // 4
    inv = ROPE_BASE ** (-jnp.arange(quarter, dtype=F32) / quarter)
    ang_r = row[:, None] * inv[None, :]
    ang_c = col[:, None] * inv[None, :]
    ang = jnp.concatenate([ang_r, ang_r, ang_c, ang_c], axis=-1)
    sign = jnp.where((jnp.arange(HEAD) % 32) < 16, -1.0, 1.0).astype(F32)
    cos = jnp.cos(ang)
    sin = jnp.sin(ang) * sign
    cos = jnp.concatenate([cos, cos], axis=-1)
    sin = jnp.concatenate([sin, sin], axis=-1)
    cos = jnp.concatenate([jnp.ones((tm, LANE), F32), cos], axis=0)
    sin = jnp.concatenate([jnp.zeros((tm, LANE), F32), sin], axis=0)
    return cos, sin


def _attn_a_kernel(lamp_ref, sub_ref, q_ref, *rest, n_ctx, n_own, tq, tk, heads, lambda_init):
    if n_ctx:
        ck_ref, cv_ref, k_ref, v_ref, o_ref, kb, vb, acc = rest
    else:
        k_ref, v_ref, o_ref, kb, vb, acc = rest
    qi = pl.program_id(2)
    n_keys = n_own + n_ctx

    @pl.when(qi == 0)
    def _():
        kb[0:n_own, :] = k_ref[...].astype(BF16)
        vb[0:n_own, :] = v_ref[...].astype(BF16)
        if n_ctx:
            kb[n_own:n_keys, :] = ck_ref[...].astype(BF16)
            vb[n_own:n_keys, :] = cv_ref[...].astype(BF16)

    lp = lamp_ref[...]
    lam = (jnp.exp(jnp.sum(lp[0:1] * lp[1:2], axis=-1, keepdims=True))
           - jnp.exp(jnp.sum(lp[2:3] * lp[3:4], axis=-1, keepdims=True)) + lambda_init)
    lo = lax.broadcasted_iota(I32, (1, LANE), 1) < HEAD
    n_main = n_own // tk
    tail = n_keys - n_main * tk

    cols = [slice(h * LANE, (h + 1) * LANE) for h in range(heads)]
    q2 = []
    for h in range(heads):
        q = q_ref[:, cols[h]] * LOG2E
        q2.append(jnp.concatenate([jnp.where(lo, q, 0.0), jnp.where(lo, 0.0, q)],
                                  axis=0).astype(BF16))
    acc[...] = jnp.zeros_like(acc)

    def step(rows, carry):
        out = []
        for h in range(heads):
            m, l = carry[h]
            s = _dot_nt(q2[h], kb[rows, cols[h]])
            m_new = jnp.maximum(m, jnp.max(s, axis=-1, keepdims=True))
            alpha = jnp.exp2(m - m_new)
            p = jnp.exp2(s - m_new)
            acc[h] = alpha * acc[h] + _dot(p.astype(BF16), vb[rows, cols[h]])
            out.append((m_new, alpha * l + jnp.sum(p, axis=-1, keepdims=True)))
        return tuple(out)

    def body(j, carry):
        return step(pl.ds(pl.multiple_of(j * tk, tk), tk), carry)

    carry = tuple((jnp.full((2 * tq, 1), NEG_INF, F32), jnp.zeros((2 * tq, 1), F32))
                  for _ in range(heads))
    if n_main:
        carry = lax.fori_loop(0, n_main, body, carry)
    if tail:
        carry = step(slice(n_main * tk, n_keys), carry)
    for h in range(heads):
        on = acc[h] / carry[h][1]
        o = on[0:tq] - lam * on[tq:2 * tq]
        o = o * lax.rsqrt(jnp.mean(o * o, axis=-1, keepdims=True) + NORM_EPS)
        o_ref[:, cols[h]] = (o * sub_ref[...] * (1.0 - lambda_init)).astype(BF16)


def _diff_attention(qkv, lamp, sub_gain, cache_k, cache_v, *, row0, n_batch, n_own, heads_total,
                    lambda_init):
    width = heads_total * LANE
    heads = 2 if heads_total % 2 == 0 else 1
    gw = heads * LANE
    n_groups = heads_total // heads
    tq = _pick(n_own, (256, 128))
    tk = _pick(n_own, (2048, 1024, 512, 256, 128))
    n_ctx = 0 if cache_k is None else cache_k.shape[0] // n_batch
    q_tiles = n_own // tq
    q_row0 = row0 // tq
    kv_row0 = row0 // n_own
    sec = width // gw

    in_specs = [
        pl.BlockSpec((4, HEAD), lambda b, g, i: (0, 0)),
        pl.BlockSpec((1, LANE), lambda b, g, i: (0, 0)),
        pl.BlockSpec((tq, gw), lambda b, g, i: (q_row0 + b * q_tiles + i, g)),
    ]
    args = [lamp, sub_gain.reshape(1, LANE), qkv]
    if n_ctx:
        in_specs += [pl.BlockSpec((n_ctx, gw), lambda b, g, i: (b, g)),
                     pl.BlockSpec((n_ctx, gw), lambda b, g, i: (b, g))]
        args += [cache_k, cache_v]
    in_specs += [pl.BlockSpec((n_own, gw), lambda b, g, i: (kv_row0 + b, sec + g)),
                 pl.BlockSpec((n_own, gw), lambda b, g, i: (kv_row0 + b, 2 * sec + g))]
    args += [qkv, qkv]
    n_keys = n_own + n_ctx
    return pl.pallas_call(
        functools.partial(_attn_a_kernel, n_ctx=n_ctx, n_own=n_own, tq=tq, tk=tk, heads=heads,
                          lambda_init=lambda_init),
        out_shape=jax.ShapeDtypeStruct((n_batch * n_own, width), BF16),
        grid=(n_batch, n_groups, q_tiles),
        in_specs=in_specs,
        out_specs=pl.BlockSpec((tq, gw), lambda b, g, i: (b * q_tiles + i, g)),
        scratch_shapes=[pltpu.VMEM((n_keys, gw), BF16), pltpu.VMEM((n_keys, gw), BF16),
                        pltpu.VMEM((heads, 2 * tq, LANE), F32)],
        compiler_params=_params(3),
        name="diff_attention_ctx" if n_ctx else "diff_attention_self",
    )(*args)


def _attn_b_kernel(sink_ref, q_ref, *rest, n_ctx, n_own, tq, windowed, slabs):
    if n_ctx:
        ck_ref, cv_ref, k_ref, v_ref, o_ref, kb, vb = rest
    else:
        k_ref, v_ref, o_ref, kb, vb = rest
    g = pl.program_id(1)
    qi = pl.program_id(2)
    pad = WINDOW if windowed else 0
    own0 = n_ctx + pad

    @pl.when(qi == 0)
    def _():
        if n_ctx:
            kb[0:n_ctx, :] = ck_ref[...].astype(BF16)
            vb[0:n_ctx, :] = cv_ref[...].astype(BF16)
        if pad:
            zeros = jnp.zeros((pad, LANE), BF16)
            for ref in (kb, vb):
                ref[n_ctx:own0, :] = zeros
                ref[own0 + n_own:own0 + n_own + pad, :] = zeros
        kb[own0:own0 + n_own, :] = k_ref[...].astype(BF16)
        vb[own0:own0 + n_own, :] = v_ref[...].astype(BF16)

    if windowed:
        wlen = tq + 2 * pad
        st = pl.multiple_of(n_ctx + qi * tq, tq)
        k_w = kb[pl.ds(st, wlen), :]
        v_w = vb[pl.ds(st, wlen), :]
        r = lax.broadcasted_iota(I32, (2 * tq, wlen), 0) & (tq - 1)
        c = lax.broadcasted_iota(I32, (2 * tq, wlen), 1)
        kpos = qi * tq - pad + c
        d = c - r
        mask = (d >= 0) & (d <= 2 * pad) & (kpos >= 0) & (kpos < n_own)
    else:
        k_w = kb[own0:own0 + n_own, :]
        v_w = vb[own0:own0 + n_own, :]
        mask = None
    if n_ctx:
        k_c = kb[0:n_ctx, :]
        v_c = vb[0:n_ctx, :]

    lo = lax.broadcasted_iota(I32, (1, LANE), 1) < HEAD
    top = lax.broadcasted_iota(I32, (2 * tq, 1), 0) < tq
    for t in range(slabs):
        cols = slice(t * LANE, (t + 1) * LANE)
        q = q_ref[:, cols] * LOG2E
        q2 = jnp.concatenate([jnp.where(lo, q, 0.0), jnp.where(lo, 0.0, q)], axis=0).astype(BF16)
        head0 = g * (2 * slabs) + 2 * t
        sk = jnp.where(top, sink_ref[head0], sink_ref[head0 + 1]) * LOG2E
        s_w = _dot_nt(q2, k_w)
        if mask is not None:
            s_w = jnp.where(mask, s_w, NEG_INF)
        m = jnp.maximum(jnp.max(s_w, axis=-1, keepdims=True), sk)
        if n_ctx:
            s_c = _dot_nt(q2, k_c)
            m = jnp.maximum(m, jnp.max(s_c, axis=-1, keepdims=True))
        p_w = jnp.exp2(s_w - m)
        den = jnp.sum(p_w, axis=-1, keepdims=True) + jnp.exp2(sk - m)
        o = _dot(p_w.astype(BF16), v_w)
        if n_ctx:
            p_c = jnp.exp2(s_c - m)
            den = den + jnp.sum(p_c, axis=-1, keepdims=True)
            o = o + _dot(p_c.astype(BF16), v_c)
        o = o / den
        o_ref[:, cols] = jnp.where(lo, o[0:tq], o[tq:2 * tq]).astype(BF16)


def _sink_attention(qkv, sinks, cache_k, cache_v, *, row0, n_batch, n_own, q_width, kv_heads,
                    windowed):
    gw = q_width // kv_heads
    slabs = gw // LANE
    tq = _pick(n_own, (256, 128))
    n_ctx = 0 if cache_k is None else cache_k.shape[0] // n_batch
    q_tiles = n_own // tq
    q_row0 = row0 // tq
    kv_row0 = row0 // n_own
    k_col0 = q_width // LANE
    v_col0 = k_col0 + kv_heads
    pad = WINDOW if windowed else 0

    in_specs = [
        pl.BlockSpec(memory_space=pltpu.SMEM),
        pl.BlockSpec((tq, gw), lambda b, g, i: (q_row0 + b * q_tiles + i, g)),
    ]
    args = [sinks, qkv]
    if n_ctx:
        in_specs += [pl.BlockSpec((n_ctx, LANE), lambda b, g, i: (b, g)),
                     pl.BlockSpec((n_ctx, LANE), lambda b, g, i: (b, g))]
        args += [cache_k, cache_v]
    in_specs += [pl.BlockSpec((n_own, LANE), lambda b, g, i: (kv_row0 + b, k_col0 + g)),
                 pl.BlockSpec((n_own, LANE), lambda b, g, i: (kv_row0 + b, v_col0 + g))]
    args += [qkv, qkv]
    n_rows = n_ctx + n_own + 2 * pad
    return pl.pallas_call(
        functools.partial(_attn_b_kernel, n_ctx=n_ctx, n_own=n_own, tq=tq, windowed=windowed,
                          slabs=slabs),
        out_shape=jax.ShapeDtypeStruct((n_batch * n_own, q_width), BF16),
        grid=(n_batch, kv_heads, q_tiles),
        in_specs=in_specs,
        out_specs=pl.BlockSpec((tq, gw), lambda b, g, i: (b * q_tiles + i, g)),
        scratch_shapes=[pltpu.VMEM((n_rows, LANE), BF16), pltpu.VMEM((n_rows, LANE), BF16)],
        compiler_params=_params(3),
        name="sink_attention_window" if windowed else "sink_attention_self",
    )(*args)


def _out_kernel(oc_ref, ol_ref, wo_ref, x_ref, gt_ref, g_ref, sh_ref, sc_ref, wr_ref, br_ref,
                x1_ref, hp_ref, route_ref, gate_ref, cnt_ref, carry, *, n_exp, tm, n_ctx_tiles):
    i = pl.program_id(0)

    @pl.when(i == 0)
    def _():
        carry[...] = jnp.zeros_like(carry)

    o = jnp.where(i < n_ctx_tiles, oc_ref[...], ol_ref[...])
    x1 = x_ref[...] + gt_ref[...] * _dot(o, wo_ref[...])
    x1_ref[...] = x1
    h = _modulate(x1, g_ref[...], sh_ref[...], sc_ref[...])
    h_hi, h_lo = _split(h)
    half = h.shape[1] // 2
    ua = pltpu.bitcast(h_hi[:, :half].astype(F32), U32)
    ub = pltpu.bitcast(h_hi[:, half:].astype(F32), U32)
    hp_ref[...] = ua | (ub >> 16)

    w_hi, w_lo = _split(wr_ref[...])
    logits = _dot(h_hi, w_hi) + _dot(h_lo, w_hi) + _dot(h_hi, w_lo) + br_ref[...]

    eio = lax.broadcasted_iota(I32, (tm, n_exp), 1).astype(F32)
    work = logits
    tops, idxs, hots = [], [], []
    for _ in range(TOP_K):
        mk = jnp.max(work, axis=-1, keepdims=True)
        ik = jnp.min(jnp.where(work == mk, eio, float(n_exp)), axis=-1, keepdims=True)
        hot = eio == ik
        tops.append(mk)
        idxs.append(ik)
        hots.append(hot)
        work = jnp.where(hot, -3e38, work)
    es = [jnp.exp(t - tops[0]) for t in tops]
    den = es[0] + es[1] + es[2] + es[3]

    hot_all = jnp.zeros((tm, n_exp), F32)
    for hot in hots:
        hot_all = hot_all + jnp.where(hot, 1.0, 0.0)
    tri = jnp.where(lax.broadcasted_iota(I32, (tm, tm), 0) > lax.broadcasted_iota(I32, (tm, tm), 1),
                    1.0, 0.0).astype(BF16)
    base = _dot(tri, hot_all.astype(BF16)) + carry[...]
    ranks = [jnp.sum(jnp.where(hot, base, 0.0), axis=-1, keepdims=True).astype(I32) for hot in hots]
    carry[...] = carry[...] + jnp.sum(hot_all, axis=0, keepdims=True)
    cnt_ref[...] = carry[...]

    col = lax.broadcasted_iota(I32, (tm, LANE), 1)
    route = jnp.zeros((tm, LANE), I32)
    gate = jnp.zeros((tm, LANE), F32)
    for k in range(TOP_K):
        route = jnp.where(col == k, idxs[k].astype(I32), route)
        route = jnp.where(col == TOP_K + k, ranks[k], route)
        gate = jnp.where(col == k, es[k] / den, gate)
    route_ref[...] = route
    gate_ref[...] = gate


def _out_project(o_ctx, o_lat, w_o, x, mod, layer, g_ffn, w_router, b_router, *, t_ctx, n_lat):
    t, d = x.shape
    kdim = o_ctx.shape[1]
    n_exp = w_router.shape[1]
    tm = _pick(math.gcd(t_ctx, n_lat), (512, 256, 128))
    n_ctx_tiles = t_ctx // tm
    cond_fn = functools.partial(_cond_of_tile, n_ctx_tiles=n_ctx_tiles, tiles_per_seq=n_lat // tm)
    row = lambda i: (i, 0)
    fixed = lambda i: (0, 0)
    return pl.pallas_call(
        functools.partial(_out_kernel, n_exp=n_exp, tm=tm, n_ctx_tiles=n_ctx_tiles),
        out_shape=(jax.ShapeDtypeStruct((t, d), F32), jax.ShapeDtypeStruct((t, d // 2), U32),
                   jax.ShapeDtypeStruct((t, LANE), I32), jax.ShapeDtypeStruct((t, LANE), F32),
                   jax.ShapeDtypeStruct((1, n_exp), F32)),
        grid=(t // tm,),
        in_specs=[
            pl.BlockSpec((tm, kdim), lambda i: (jnp.minimum(i, n_ctx_tiles - 1), 0)),
            pl.BlockSpec((tm, kdim), lambda i: (jnp.maximum(i - n_ctx_tiles, 0), 0)),
            pl.BlockSpec((kdim, d), fixed),
            pl.BlockSpec((tm, d), row),
            _mod_spec(layer, 2, d, cond_fn),
            pl.BlockSpec((1, d), fixed),
            _mod_spec(layer, 3, d, cond_fn),
            _mod_spec(layer, 4, d, cond_fn),
            pl.BlockSpec((d, n_exp), fixed),
            pl.BlockSpec((1, n_exp), fixed),
        ],
        out_specs=(pl.BlockSpec((tm, d), row), pl.BlockSpec((tm, d // 2), row),
                   pl.BlockSpec((tm, LANE), row), pl.BlockSpec((tm, LANE), row),
                   pl.BlockSpec((1, n_exp), fixed)),
        scratch_shapes=[pltpu.VMEM((1, n_exp), F32)],
        compiler_params=_params(1),
        name=f"out_project_route_{layer}",
    )(o_ctx, o_lat, w_o, x, mod, g_ffn.reshape(1, d), mod, mod, w_router,
      b_router.reshape(1, n_exp))


def _moe_kernel(te_ref, nv_ref, tok_ref, hp_hbm, wg_ref, wu_ref, bg_ref, bu_ref, wo_ref, bo_ref,
                y_ref, xs, xb, act, gsem, *, tm, nf1, rows_per_step):
    i = pl.program_id(0)
    s = pl.program_id(1)
    nv = nv_ref[0]
    cur = i % 2
    nxt = 1 - cur

    row_words = xs.shape[2]

    def row_copy(buf, tile, r):
        start = pl.multiple_of(tok_ref[tile * tm + r], row_words)
        return pltpu.make_async_copy(hp_hbm.at[pl.ds(start, row_words)], xs.at[buf, r], gsem.at[buf])

    @pl.when(jnp.logical_and(i == 0, s == 0))
    def _():
        def issue(r, c):
            row_copy(0, 0, r).start(priority=GATHER_PRIORITY)
            return c

        lax.fori_loop(0, tm, issue, 0)

    @pl.when(jnp.logical_and(i < nv, s == 0))
    def _():
        pltpu.make_async_copy(xs.at[nxt], xs.at[cur], gsem.at[cur]).wait()
        u = xs[cur]
        xb[:, :row_words] = pltpu.bitcast(u & jnp.uint32(0xFFFF0000), F32).astype(BF16)
        xb[:, row_words:] = pltpu.bitcast(u << 16, F32).astype(BF16)

    def activation_chunk():
        wg = wg_ref[...].astype(BF16)
        wu = wu_ref[...].astype(BF16)
        for r in range(2):
            rows = slice(r * (tm // 2), (r + 1) * (tm // 2))
            x = xb[rows, :]
            g = _dot(x, wg) + bg_ref[...]
            u = _dot(x, wu) + bu_ref[...]
            g = jnp.minimum(g, SWIGLU_LIMIT)
            u = jnp.clip(u, -SWIGLU_LIMIT, SWIGLU_LIMIT)
            act[s, rows, :] = (g * jax.nn.sigmoid(SWIGLU_ALPHA * g) * (u + 1.0)).astype(BF16)

    @pl.when(jnp.logical_and(i + 1 < nv, s < nf1))
    def _():
        r0 = s * rows_per_step
        for r in range(rows_per_step):
            row_copy(nxt, i + 1, r0 + r).start(priority=GATHER_PRIORITY)
        activation_chunk()

    @pl.when(jnp.logical_and(i + 1 == nv, s < nf1))
    def _():
        activation_chunk()

    @pl.when(jnp.logical_and(i < nv, s >= nf1))
    def _():
        a = jnp.concatenate([act[c] for c in range(nf1)], axis=1)
        y_ref[...] = _dot(a, wo_ref[...].astype(BF16)) + bo_ref[...]

    @pl.when(jnp.logical_and(i >= nv, s >= nf1))
    def _():
        y_ref[...] = jnp.zeros_like(y_ref)


def _moe_experts(hp, slot_token, tile_expert, n_valid, layer, w_in, b_in, w_out, b_out, *, tm):
    cap = slot_token.shape[0]
    depth, n_exp, d, two_ff = w_in.shape
    d_ff = two_ff // 2
    tf = _pick(d_ff, (256, 128))
    nf1 = d_ff // tf
    tn = _pick(d, (512, 256, 128))
    nf2 = d // tn
    row_words = hp.shape[1]
    rows_per_step = tm // nf1

    def in_chunk(i, s, nv):
        return jnp.where(i < nv[0], jnp.minimum(s, nf1 - 1), nf1 - 1)

    def out_chunk(i, s, nv):
        return jnp.where(i < nv[0], jnp.maximum(s - nf1, 0), nf2 - 1)

    def wo_idx(i, s, te, nv, tok):
        hold = jnp.logical_and(s == 0, i > 0)
        e = jnp.where(hold, te[jnp.maximum(i - 1, 0)], te[i])
        return (layer, e, 0, jnp.where(hold, nf2 - 1, out_chunk(i, s, nv)))

    wblk = (None, None, d, tf)
    bblk = (None, None, 1, tf)
    grid_spec = pltpu.PrefetchScalarGridSpec(
        num_scalar_prefetch=3,
        grid=(cap // tm, nf1 + nf2),
        in_specs=[
            pl.BlockSpec(memory_space=pl.ANY),
            pl.BlockSpec(wblk, lambda i, s, te, nv, tok: (layer, te[i], 0, in_chunk(i, s, nv))),
            pl.BlockSpec(wblk,
                         lambda i, s, te, nv, tok: (layer, te[i], 0, nf1 + in_chunk(i, s, nv))),
            pl.BlockSpec(bblk, lambda i, s, te, nv, tok: (layer, te[i], 0, in_chunk(i, s, nv))),
            pl.BlockSpec(bblk,
                         lambda i, s, te, nv, tok: (layer, te[i], 0, nf1 + in_chunk(i, s, nv))),
            pl.BlockSpec((None, None, d_ff, tn), wo_idx),
            pl.BlockSpec((None, None, 1, tn),
                         lambda i, s, te, nv, tok: (layer, te[i], 0, out_chunk(i, s, nv))),
        ],
        out_specs=pl.BlockSpec((tm, tn), lambda i, s, te, nv, tok: (i, jnp.maximum(s - nf1, 0))),
        scratch_shapes=[pltpu.VMEM((2, tm, row_words), U32), pltpu.VMEM((tm, d), BF16),
                        pltpu.VMEM((nf1, tm, tf), BF16), pltpu.SemaphoreType.DMA((2,))],
    )
    b_in4 = b_in.reshape(depth, n_exp, 1, two_ff)
    return pl.pallas_call(
        functools.partial(_moe_kernel, tm=tm, nf1=nf1, rows_per_step=rows_per_step),
        out_shape=jax.ShapeDtypeStruct((cap, d), F32),
        grid_spec=grid_spec,
        compiler_params=_params(2),
        name=f"moe_experts_{layer}",
    )(tile_expert, n_valid, slot_token * row_words, hp.reshape(-1),
      w_in, w_in, b_in4, b_in4, w_out, b_out.reshape(depth, n_exp, 1, d))


def _combine_kernel(slot_ref, gate_ref, x_ref, gt_ref, y_hbm, o_ref, buf, sem, *, tc):
    i = pl.program_id(0)
    n = pl.num_programs(0)
    cur = i % 2
    nxt = 1 - cur

    def start_tile(tile, b):
        base = tile * (tc * TOP_K)

        def issue(r, c):
            for k in range(TOP_K):
                pltpu.make_async_copy(y_hbm.at[pl.ds(slot_ref[base + r * TOP_K + k], 1)],
                                      buf.at[b, k, pl.ds(r, 1)], sem.at[b]).start(priority=k % 2)
            return c

        lax.fori_loop(0, tc, issue, 0)

    @pl.when(i == 0)
    def _():
        start_tile(0, 0)

    @pl.when(i + 1 < n)
    def _():
        start_tile(i + 1, nxt)

    pltpu.make_async_copy(buf.at[nxt], buf.at[cur], sem.at[cur]).wait()
    gate = gate_ref[...]
    f = gate[:, 0:1] * buf[cur, 0]
    for k in range(1, TOP_K):
        f = f + gate[:, k:k + 1] * buf[cur, k]
    o_ref[...] = x_ref[...] + gt_ref[...] * f


def _moe_combine(slot, gates, x1, mod, layer, y_sorted, *, t_ctx, n_lat):
    t, d = x1.shape
    tc = _pick(math.gcd(t_ctx, n_lat), (128,))
    cond_fn = functools.partial(_cond_of_tile, n_ctx_tiles=t_ctx // tc, tiles_per_seq=n_lat // tc)
    row = lambda i, sl: (i, 0)
    grid_spec = pltpu.PrefetchScalarGridSpec(
        num_scalar_prefetch=1,
        grid=(t // tc,),
        in_specs=[
            pl.BlockSpec((tc, LANE), row),
            pl.BlockSpec((tc, d), row),
            _mod_spec(layer, 5, d, cond_fn),
            pl.BlockSpec(memory_space=pl.ANY),
        ],
        out_specs=pl.BlockSpec((tc, d), row),
        scratch_shapes=[pltpu.VMEM((2, TOP_K, tc, d), F32), pltpu.SemaphoreType.DMA((2,))],
    )
    return pl.pallas_call(
        functools.partial(_combine_kernel, tc=tc),
        out_shape=jax.ShapeDtypeStruct((t, d), F32),
        grid_spec=grid_spec,
        compiler_params=_params(1),
        name="moe_combine",
    )(slot.reshape(-1), gates, x1, mod, y_sorted)


def _moe_tile(n_assign, n_exp):
    per_expert = n_assign // n_exp
    for tm in (1024, 512, 256):
        if per_expert >= 3 * tm:
            return tm
    return 128


def _moe_ffn(hp, route, gates, counts, x1, mod, layer, w_in, b_in, w_out, b_out, *, t_ctx, n_lat):
    t = x1.shape[0]
    n_exp = w_in.shape[1]
    n_assign = t * TOP_K
    tm = _moe_tile(n_assign, n_exp)
    cap = -(-(n_assign + n_exp * (tm - 1)) // tm) * tm
    n_tiles = cap // tm

    counts = counts.reshape(n_exp).astype(I32)
    padded = (counts + tm - 1) // tm * tm
    pend = jnp.cumsum(padded)
    pstart = pend - padded
    idx = route[:, :TOP_K]
    slot = pstart[idx] + route[:, TOP_K:2 * TOP_K]
    token_of = jnp.broadcast_to(jnp.arange(t, dtype=I32)[:, None], (t, TOP_K))
    slot_token = jnp.zeros((cap,), I32).at[slot.reshape(-1)].set(
        token_of.reshape(-1), unique_indices=True)
    n_valid = (pend[-1] // tm).astype(I32)
    tile_start = jnp.minimum(jnp.arange(n_tiles, dtype=I32), n_valid - 1) * tm
    tile_expert = jnp.minimum(jnp.searchsorted(pend, tile_start, side='right'),
                              n_exp - 1).astype(I32)

    y_sorted = _moe_experts(hp, slot_token, tile_expert, n_valid.reshape(1), layer, w_in, b_in,
                            w_out, b_out, tm=tm)
    return _moe_combine(slot, gates, x1, mod, layer, y_sorted, t_ctx=t_ctx, n_lat=n_lat)


def _head_table(gain, n_heads, scale=1.0):
    return jnp.tile(gain * scale, n_heads)


def kernel(x_prompt, x_sample, cache_a_k, cache_a_v, cache_b_k, cache_b_v, c, c_ctx, w_ada, b_ada, g_mix, g_ffn, w_qkv_a, q_norm_a, k_norm_a, lam_q1_a, lam_k1_a, lam_q2_a, lam_k2_a, subln_a, w_o_a, w_qkv_b, q_norm_b, k_norm_b, sinks_b, w_o_b, w_router, b_router, w_in, b_in, w_out, b_out):
    batch, seq, d = x_prompt.shape
    dec_batch, n_lat, _ = x_sample.shape
    depth = w_ada.shape[0]
    past = cache_a_k.shape[2]
    heads_a = cache_a_k.shape[3]
    kv_heads_b = cache_b_k.shape[3]
    q_heads_b = sinks_b.shape[1]
    t_ctx = batch * seq
    t_lat = dec_batch * n_lat
    assert dec_batch + 1 <= N_COND

    x = jnp.concatenate([x_prompt.reshape(t_ctx, d), x_sample.reshape(t_lat, d)], axis=0)
    cond = jnp.concatenate([c_ctx[None, :], c, jnp.zeros((N_COND - 1 - dec_batch, d), F32)], axis=0)
    mod = _ada_params(cond, w_ada, b_ada).reshape(depth, N_COND, 6, 1, d)

    tm_qkv = _pick(math.gcd(t_ctx, n_lat), (1024, 512, 256, 128))
    cos_tab, sin_tab = _rope_tables(n_lat, tm_qkv)
    grp = jnp.where((jnp.arange(LANE)[:, None] // HEAD) == (jnp.arange(LANE)[None, :] // HEAD),
                    1.0 / HEAD, 0.0).astype(BF16)

    wa = heads_a * LANE
    qw_b = q_heads_b * HEAD
    kvw_b = kv_heads_b * HEAD
    states = {"a_k": [], "a_v": [], "b_k": [], "b_v": []}

    for layer in range(depth):
        i = layer // 2
        if layer % 2 == 0:
            lambda_init = 0.8 - 0.6 * math.exp(-0.3 * layer)
            gain = jnp.concatenate([
                _head_table(q_norm_a[i], 2 * heads_a, HEAD ** -0.5),
                _head_table(k_norm_a[i], 2 * heads_a),
                jnp.ones((wa,), F32)]).reshape(1, 3 * wa)
            qkv = _qkv_project(x, mod, layer, g_mix[layer], w_qkv_a[i].astype(BF16), gain,
                               cos_tab, sin_tab, grp, tm=tm_qkv, n_norm=2 * wa, t_ctx=t_ctx, n_lat=n_lat)
            states["a_k"].append(qkv[:t_ctx, wa:2 * wa].reshape(batch, seq, heads_a, 2, HEAD))
            states["a_v"].append(qkv[:t_ctx, 2 * wa:].reshape(batch, seq, heads_a, LANE))
            lamp = jnp.stack([lam_q1_a[i], lam_k1_a[i], lam_q2_a[i], lam_k2_a[i]])
            o_p = _diff_attention(qkv, lamp, subln_a[i], None, None, row0=0, n_batch=batch,
                                  n_own=seq, heads_total=heads_a, lambda_init=lambda_init)
            o_s = _diff_attention(qkv, lamp, subln_a[i],
                                  cache_a_k[:, i].reshape(dec_batch * past, wa),
                                  cache_a_v[:, i].reshape(dec_batch * past, wa),
                                  row0=t_ctx, n_batch=dec_batch, n_own=n_lat, heads_total=heads_a,
                                  lambda_init=lambda_init)
            w_o = w_o_a[i]
        else:
            wq, wk, wv = jnp.split(w_qkv_b[i], [qw_b, qw_b + kvw_b], axis=1)
            dup = lambda w: jnp.concatenate([w.reshape(d, kv_heads_b, 1, HEAD)] * 2, axis=2).reshape(
                d, 2 * kvw_b)
            w_b = jnp.concatenate([wq, dup(wk), dup(wv)], axis=1).astype(BF16)
            gain = jnp.concatenate([
                _head_table(q_norm_b[i], q_heads_b, HEAD ** -0.5),
                _head_table(k_norm_b[i], 2 * kv_heads_b),
                jnp.ones((2 * kvw_b,), F32)]).reshape(1, qw_b + 4 * kvw_b)
            qkv = _qkv_project(x, mod, layer, g_mix[layer], w_b, gain, cos_tab, sin_tab, grp,
                               tm=tm_qkv, n_norm=qw_b + 2 * kvw_b, t_ctx=t_ctx, n_lat=n_lat)
            k_state = qkv[:t_ctx, qw_b:qw_b + 2 * kvw_b].reshape(batch, seq, kv_heads_b, 2, HEAD)
            v_state = qkv[:t_ctx, qw_b + 2 * kvw_b:].reshape(batch, seq, kv_heads_b, 2, HEAD)
            states["b_k"].append(k_state[:, :, :, 0])
            states["b_v"].append(v_state[:, :, :, 0])
            dup_c = lambda a: jnp.concatenate([a, a], axis=-1).reshape(dec_batch * past,
                                                                       2 * kvw_b)
            o_p = _sink_attention(qkv, sinks_b[i], None, None, row0=0, n_batch=batch, n_own=seq,
                                  q_width=qw_b, kv_heads=kv_heads_b, windowed=False)
            o_s = _sink_attention(qkv, sinks_b[i], dup_c(cache_b_k[:, i]), dup_c(cache_b_v[:, i]),
                                  row0=t_ctx, n_batch=dec_batch, n_own=n_lat, q_width=qw_b,
                                  kv_heads=kv_heads_b, windowed=True)
            w_o = w_o_b[i]

        x1, hp, route, gates, counts = _out_project(
            o_p, o_s, w_o.astype(BF16), x, mod, layer, g_ffn[layer], w_router[layer],
            b_router[layer], t_ctx=t_ctx, n_lat=n_lat)
        x = _moe_ffn(hp, route, gates, counts, x1, mod, layer, w_in, b_in, w_out, b_out,
                     t_ctx=t_ctx, n_lat=n_lat)

    return (x[:t_ctx].reshape(batch, seq, d), x[t_ctx:].reshape(dec_batch, n_lat, d),
            jnp.stack(states["a_k"], axis=1), jnp.stack(states["a_v"], axis=1),
            jnp.stack(states["b_k"], axis=1), jnp.stack(states["b_v"], axis=1))
```
